```python
import jax, jax.numpy as jnp
from jax import lax
import numpy as np

D_MODEL = 2048
BATCH = 1
SEQ = 8192
DEPTH = 2
DEC_BATCH = 16
DEC_SEQ = 64
PAST_LEN = 4096

CHUNK = 64
N_EVEN = (DEPTH + 1) // 2
N_ODD = DEPTH // 2
D_A = D_MODEL // 2
SC_WIDTH = 3
D_B = D_MODEL // 2
SGU_BLOCK = 128
SGU_HEAD_DIM = 128
SGU_HEADS = D_B // SGU_HEAD_DIM
D_C = D_MODEL
CONF_WIDTH = 31
N_MEM = 256
MEM_HEADS = 4
MEM_HEAD_DIM = D_MODEL // MEM_HEADS
N_GROUPS = 4
EXPERTS_PER_GROUP = 4
N_EXPERTS = N_GROUPS * EXPERTS_PER_GROUP
TOP_K = 2
D_EXPERT = 512
LN_EPS = 1e-5
DEEPNORM_ALPHA = (2.0 * DEPTH) ** 0.25
DEEPNORM_BETA = (8.0 * DEPTH) ** -0.25

kernel_name = "hybrid_streaming_encoder_step"


def layer_norm(x, g, b):
    xf = x.astype(jnp.float32)
    mu = jnp.mean(xf, axis=-1, keepdims=True)
    var = jnp.mean(jnp.square(xf - mu), axis=-1, keepdims=True)
    y = (xf - mu) * lax.rsqrt(var + LN_EPS) * g.astype(jnp.float32) + b.astype(jnp.float32)
    return y.astype(x.dtype)


def causal_depthwise_conv(x_ctx, w):
    c = w.shape[1]
    return lax.conv_general_dilated(
        x_ctx, w[:, None, :].astype(x_ctx.dtype), window_strides=(1,), padding="VALID",
        dimension_numbers=("NWC", "WIO", "NWC"), feature_group_count=c)


def sgu_mask():
    pos = jnp.arange(SGU_BLOCK)
    return (pos[None, :] // CHUNK) <= (pos[:, None] // CHUNK)


def spatial_gating(u, v, w_s, b_s):
    n, t, _ = v.shape
    pad = (-t) % SGU_BLOCK
    nb = (t + pad) // SGU_BLOCK
    v_blk = jnp.pad(v, ((0, 0), (0, pad), (0, 0))).reshape(n, nb, SGU_BLOCK, SGU_HEADS, SGU_HEAD_DIM)
    w_m = jnp.where(sgu_mask()[None], w_s, 0)
    f = jnp.einsum("hij,nbjhd->nbihd", w_m, v_blk) + b_s.T[None, None, :, :, None]
    f = f.reshape(n, nb * SGU_BLOCK, D_B)[:, :t]
    return u * f


def memory_attention(x, mem_k, mem_v, w_q, w_o):
    n, t, _ = x.shape
    q = (x @ w_q).reshape(n, t, MEM_HEADS, MEM_HEAD_DIM)
    s = jnp.einsum("nthe,nshe->nhts", q, mem_k).astype(jnp.float32) * (MEM_HEAD_DIM ** -0.5)
    p = jax.nn.softmax(s, axis=-1).astype(x.dtype)
    o = jnp.einsum("nhts,nshe->nthe", p, mem_v).reshape(n, t, D_MODEL)
    return o @ w_o


def hierarchical_moe(x, w_rg, b_rg, w_re, b_re, w_gate, w_up, w_down):
    n, t, d = x.shape
    xt = x.reshape(n * t, d)
    g_logits = (xt @ w_rg).astype(jnp.float32) + b_rg.astype(jnp.float32)
    g_prob = jax.nn.softmax(g_logits, axis=-1)
    g_idx = jnp.argmax(g_logits, axis=-1)
    p_group = jnp.take_along_axis(g_prob, g_idx[:, None], axis=-1)
    e_logits = jnp.einsum("md,gde->mge", xt, w_re).astype(jnp.float32) + b_re.astype(jnp.float32)
    e_logits = jnp.take_along_axis(e_logits, g_idx[:, None, None], axis=1)[:, 0]
    top_vals, top_idx = lax.top_k(e_logits, TOP_K)
    weights = jax.nn.softmax(top_vals, axis=-1) * p_group
    expert_id = g_idx[:, None] * EXPERTS_PER_GROUP + top_idx
    combine = jnp.sum(jax.nn.one_hot(expert_id, N_EXPERTS, dtype=jnp.float32) * weights[..., None], axis=1)
    h = jax.nn.silu(jnp.einsum("md,edf->mef", xt, w_gate)) * jnp.einsum("md,edf->mef", xt, w_up)
    y = jnp.einsum("mef,efd->md", h * combine.astype(h.dtype)[..., None], w_down)
    return y.reshape(n, t, d)


def trunk(x, mem_k, mem_v, conv3_buf, conv31_buf, p):
    new_c3, new_c31, sgu_rows = [], [], []
    for l in range(DEPTH):
        i = l // 2
        if l % 2 == 0:
            proj = x @ p["w_in_ab"][i]
            h_a, gb_a, gc_a, uv_b = jnp.split(proj, [D_A, 2 * D_A, 3 * D_A], axis=-1)
            ctx = jnp.concatenate([conv3_buf[i], gc_a * h_a], axis=1)
            new_c3.append(ctx[:, -(SC_WIDTH - 1):])
            y_a = gb_a * causal_depthwise_conv(ctx, p["w_conv3"][i])
            u_b, v_b = jnp.split(jax.nn.gelu(uv_b), 2, axis=-1)
            v_b = layer_norm(v_b, p["ln_sgu_gain"][i], p["ln_sgu_bias"][i])
            sgu_rows.append(v_b)
            y_b = spatial_gating(u_b, v_b, p["w_spatial"][i], p["b_spatial"][i])
            y_mix = jnp.concatenate([y_a, y_b], axis=-1) @ p["w_out_ab"][i]
        else:
            a, g = jnp.split(x @ p["w_pw1"][i], 2, axis=-1)
            ctx = jnp.concatenate([conv31_buf[i], a * jax.nn.sigmoid(g)], axis=1)
            new_c31.append(ctx[:, -(CONF_WIDTH - 1):])
            c = causal_depthwise_conv(ctx, p["w_dw31"][i]) + p["b_dw31"][i]
            c = layer_norm(c, p["ln_conf_gain"][i], p["ln_conf_bias"][i])
            y_mix = jax.nn.silu(c) @ p["w_pw2"][i]
        x = layer_norm(DEEPNORM_ALPHA * x + y_mix, p["ln_gain"][l, 0], p["ln_bias"][l, 0])
        y_mem = memory_attention(x, mem_k[l], mem_v[l], p["w_mem_q"][l], p["w_mem_o"][l])
        x = layer_norm(DEEPNORM_ALPHA * x + y_mem, p["ln_gain"][l, 1], p["ln_bias"][l, 1])
        y_ff = hierarchical_moe(x, p["w_route_group"][l], p["b_route_group"][l],
                                p["w_route_expert"][l], p["b_route_expert"][l],
                                p["w_exp_gate"][l], p["w_exp_up"][l], p["w_exp_down"][l])
        x = layer_norm(DEEPNORM_ALPHA * x + y_ff, p["ln_gain"][l, 2], p["ln_bias"][l, 2])
    return x, jnp.stack(new_c3), jnp.stack(new_c31), sgu_rows


def setup_inputs(seed: int = 0) -> dict:
    key = jax.random.key(seed)
    ks = jax.random.split(key, 40)

    def nrm(k, shape, scale):
        return jax.random.normal(k, shape, dtype=jnp.float32) * scale

    inv = lambda n: n ** -0.5
    return {
        "x_prompt": nrm(ks[0], (BATCH, SEQ, D_MODEL), 1.0),
        "x_sample": nrm(ks[1], (DEC_BATCH, DEC_SEQ, D_MODEL), 1.0),
        "mem_prompt": nrm(ks[2], (BATCH, N_MEM, D_MODEL), 1.0),
        "cache_mem_k": nrm(ks[3], (DEPTH, DEC_BATCH, N_MEM, MEM_HEADS, MEM_HEAD_DIM), 1.0),
        "cache_mem_v": nrm(ks[4], (DEPTH, DEC_BATCH, N_MEM, MEM_HEADS, MEM_HEAD_DIM), DEEPNORM_BETA),
        "state_conv3": nrm(ks[5], (N_EVEN, DEC_BATCH, SC_WIDTH - 1, D_A), 1.0),
        "state_conv31": nrm(ks[6], (N_ODD, DEC_BATCH, CONF_WIDTH - 1, D_C), 0.5),
        "ln_gain": 1.0 + nrm(ks[7], (DEPTH, 3, D_MODEL), 0.02),
        "ln_bias": nrm(ks[8], (DEPTH, 3, D_MODEL), 0.02),
        "w_in_ab": nrm(ks[9], (N_EVEN, D_MODEL, 3 * D_A + 2 * D_B), inv(D_MODEL)),
        "w_conv3": nrm(ks[10], (N_EVEN, SC_WIDTH, D_A), inv(SC_WIDTH)),
        "ln_sgu_gain": 1.0 + nrm(ks[11], (N_EVEN, D_B), 0.02),
        "ln_sgu_bias": nrm(ks[12], (N_EVEN, D_B), 0.02),
        "w_spatial": nrm(ks[13], (N_EVEN, SGU_HEADS, SGU_BLOCK, SGU_BLOCK), inv(SGU_BLOCK)),
        "b_spatial": 1.0 + nrm(ks[14], (N_EVEN, SGU_HEADS, SGU_BLOCK), 0.02),
        "w_out_ab": nrm(ks[15], (N_EVEN, D_A + D_B, D_MODEL), inv(D_A + D_B) * DEEPNORM_BETA),
        "w_pw1": nrm(ks[16], (N_ODD, D_MODEL, 2 * D_C), inv(D_MODEL)),
        "w_dw31": nrm(ks[17], (N_ODD, CONF_WIDTH, D_C), inv(CONF_WIDTH)),
        "b_dw31": nrm(ks[18], (N_ODD, D_C), 0.02),
        "ln_conf_gain": 1.0 + nrm(ks[19], (N_ODD, D_C), 0.02),
        "ln_conf_bias": nrm(ks[20], (N_ODD, D_C), 0.02),
        "w_pw2": nrm(ks[21], (N_ODD, D_C, D_MODEL), inv(D_C) * DEEPNORM_BETA),
        "w_mem_q": nrm(ks[22], (DEPTH, D_MODEL, MEM_HEADS * MEM_HEAD_DIM), inv(D_MODEL)),
        "w_mem_k": nrm(ks[23], (DEPTH, D_MODEL, MEM_HEADS * MEM_HEAD_DIM), inv(D_MODEL)),
        "w_mem_v": nrm(ks[24], (DEPTH, D_MODEL, MEM_HEADS * MEM_HEAD_DIM), inv(D_MODEL) * DEEPNORM_BETA),
        "w_mem_o": nrm(ks[25], (DEPTH, MEM_HEADS * MEM_HEAD_DIM, D_MODEL), inv(D_MODEL) * DEEPNORM_BETA),
        "w_route_group": nrm(ks[26], (DEPTH, D_MODEL, N_GROUPS), inv(D_MODEL)),
        "b_route_group": nrm(ks[27], (DEPTH, N_GROUPS), 0.01),
        "w_route_expert": nrm(ks[28], (DEPTH, N_GROUPS, D_MODEL, EXPERTS_PER_GROUP), inv(D_MODEL)),
        "b_route_expert": nrm(ks[29], (DEPTH, N_GROUPS, EXPERTS_PER_GROUP), 0.01),
        "w_exp_gate": nrm(ks[30], (DEPTH, N_EXPERTS, D_MODEL, D_EXPERT), inv(D_MODEL)),
        "w_exp_up": nrm(ks[31], (DEPTH, N_EXPERTS, D_MODEL, D_EXPERT), inv(D_MODEL)),
        "w_exp_down": nrm(ks[32], (DEPTH, N_EXPERTS, D_EXPERT, D_MODEL), inv(D_EXPERT) * DEEPNORM_BETA),
    }


def reference(x_prompt, x_sample, mem_prompt, cache_mem_k, cache_mem_v, state_conv3, state_conv31,
              ln_gain, ln_bias, w_in_ab, w_conv3, ln_sgu_gain, ln_sgu_bias, w_spatial, b_spatial, w_out_ab,
              w_pw1, w_dw31, b_dw31, ln_conf_gain, ln_conf_bias, w_pw2,
              w_mem_q, w_mem_k, w_mem_v, w_mem_o,
              w_route_group, b_route_group, w_route_expert, b_route_expert,
              w_exp_gate, w_exp_up, w_exp_down):
    params = {
        "ln_gain": ln_gain, "ln_bias": ln_bias,
        "w_in_ab": w_in_ab, "w_conv3": w_conv3, "ln_sgu_gain": ln_sgu_gain, "ln_sgu_bias": ln_sgu_bias,
        "w_spatial": w_spatial, "b_spatial": b_spatial, "w_out_ab": w_out_ab,
        "w_pw1": w_pw1, "w_dw31": w_dw31, "b_dw31": b_dw31,
        "ln_conf_gain": ln_conf_gain, "ln_conf_bias": ln_conf_bias, "w_pw2": w_pw2,
        "w_mem_q": w_mem_q, "w_mem_o": w_mem_o,
        "w_route_group": w_route_group, "b_route_group": b_route_group,
        "w_route_expert": w_route_expert, "b_route_expert": b_route_expert,
        "w_exp_gate": w_exp_gate, "w_exp_up": w_exp_up, "w_exp_down": w_exp_down,
    }
    n_p = x_prompt.shape[0]
    prompt_mem_k = jnp.einsum("nsd,lde->lnse", mem_prompt, w_mem_k).reshape(DEPTH, n_p, N_MEM, MEM_HEADS, MEM_HEAD_DIM)
    prompt_mem_v = jnp.einsum("nsd,lde->lnse", mem_prompt, w_mem_v).reshape(DEPTH, n_p, N_MEM, MEM_HEADS, MEM_HEAD_DIM)
    zero_c3 = jnp.zeros((N_EVEN, n_p, SC_WIDTH - 1, D_A), dtype=x_prompt.dtype)
    zero_c31 = jnp.zeros((N_ODD, n_p, CONF_WIDTH - 1, D_C), dtype=x_prompt.dtype)
    y_prompt, prompt_conv3, prompt_conv31, _ = trunk(x_prompt, prompt_mem_k, prompt_mem_v, zero_c3, zero_c31, params)
    y_sample, sample_conv3, sample_conv31, sample_rows = trunk(x_sample, cache_mem_k, cache_mem_v,
                                                               state_conv3, state_conv31, params)
    sample_sgu_v = jnp.stack(sample_rows)
    return (y_prompt, y_sample, prompt_mem_k, prompt_mem_v, prompt_conv3, prompt_conv31,
            sample_conv3, sample_conv31, sample_sgu_v)
```

```python
import functools

import jax
import jax.numpy as jnp
from jax import lax
from jax.experimental import pallas as pl
from jax.experimental.pallas import tpu as pltpu

F32 = jnp.float32
BF16 = jnp.bfloat16

D = 2048
N_PROMPT = 8192
N_STREAMS = 16
STEP = 64
NP_STEPS = N_PROMPT // STEP
M_TOK = N_PROMPT + N_STREAMS * STEP
N_MEM = 256
HEADS = 4
HEAD_DIM = D // HEADS
D_A = 1024
D_B = 1024
SGU_HEADS = 8
N_GROUPS = 4
E_PER = 4
N_EXPERTS = 16
D_EXPERT = 512
LN_EPS = 1e-5
ALPHA = (2.0 * 2) ** 0.25
ATTN_SCALE = HEAD_DIM ** -0.5
ROUTE_LANES = 128
NEG = -1e30

VMEM_LIMIT_BYTES = 60000 * 1024


def _cparams(n_axes):
    return pltpu.CompilerParams(
        dimension_semantics=("arbitrary",) * n_axes,
        vmem_limit_bytes=VMEM_LIMIT_BYTES)


def _ln(x, g, b):
    mu = jnp.mean(x, axis=-1, keepdims=True)
    xc = x - mu
    var = jnp.mean(xc * xc, axis=-1, keepdims=True)
    return xc * lax.rsqrt(var + LN_EPS) * g + b


def _bdot(a, b):
    return jnp.dot(a.astype(BF16), b.astype(BF16), preferred_element_type=F32)


def _mm_body(x_ref, w_ref, o_ref):
    o_ref[...] = _bdot(x_ref[...], w_ref[...]).astype(o_ref.dtype)


def _mm(x, w, layer, out_dtype, tm, tn, name):
    m, k = x.shape
    n = w.shape[-1]
    return pl.pallas_call(
        _mm_body,
        grid=(m // tm, n // tn),
        in_specs=[pl.BlockSpec((tm, k), lambda i, j: (i, 0)),
                  pl.BlockSpec((None, k, tn), lambda i, j: (layer, 0, j))],
        out_specs=pl.BlockSpec((tm, tn), lambda i, j: (i, j)),
        out_shape=jax.ShapeDtypeStruct((m, n), out_dtype),
        compiler_params=_cparams(2),
        name=name,
    )(x, w)


def _mem_kv(mem, w, name):
    depth, k, n = w.shape
    tn = 512
    return pl.pallas_call(
        _mm_body,
        grid=(depth, n // tn),
        in_specs=[pl.BlockSpec((N_MEM, k), lambda l, j: (0, 0)),
                  pl.BlockSpec((None, k, tn), lambda l, j: (l, 0, j))],
        out_specs=pl.BlockSpec((None, N_MEM, tn), lambda l, j: (l, 0, j)),
        out_shape=jax.ShapeDtypeStruct((depth, N_MEM, n), F32),
        compiler_params=_cparams(2),
        name=name,
    )(mem, w)


def _mm_ln_body(x_ref, w_ref, r_ref, g_ref, b_ref, o_ref, acc_ref, *, nj, tn):
    j = pl.program_id(1)
    acc_ref[j] = _bdot(x_ref[...], w_ref[...])

    @pl.when(j == nj - 1)
    def _():
        parts = [ALPHA * r_ref[:, k * tn:(k + 1) * tn] + acc_ref[k] for k in range(nj)]
        y = jnp.concatenate(parts, axis=-1)
        o_ref[...] = _ln(y, g_ref[...], b_ref[...])


def _mm_ln(x, w, layer, res, g, b, name, tm=512, tn=512):
    m, k = x.shape
    n = w.shape[-1]
    nj = n // tn
    return pl.pallas_call(
        functools.partial(_mm_ln_body, nj=nj, tn=tn),
        grid=(m // tm, nj),
        in_specs=[pl.BlockSpec((tm, k), lambda i, j: (i, 0)),
                  pl.BlockSpec((None, k, tn), lambda i, j: (layer, 0, j)),
                  pl.BlockSpec((tm, n), lambda i, j: (i, 0)),
                  pl.BlockSpec((1, n), lambda i, j: (0, 0)),
                  pl.BlockSpec((1, n), lambda i, j: (0, 0))],
        out_specs=pl.BlockSpec((tm, n), lambda i, j: (i, 0)),
        out_shape=jax.ShapeDtypeStruct((m, n), F32),
        scratch_shapes=[pltpu.VMEM((nj, tm, tn), F32)],
        compiler_params=_cparams(2),
        name=name,
    )(x, w, res, g, b)


def _state_block(i):
    return jnp.maximum(i - NP_STEPS, 0)


def _new_state_block(i):
    return jnp.where(i < NP_STEPS, 0, i - NP_STEPS + 1)


def _mix0_body(p_ref, st_ref, wc_ref, g_ref, b_ref, ws_ref, bs_ref,
               y_ref, c3_ref, vrow_ref, ctx_ref, vprev_ref):
    i = pl.program_id(0)

    z = p_ref[:, 2 * D_A:3 * D_A] * p_ref[:, 0:D_A]

    @pl.when(i == 0)
    def _():
        ctx_ref[0:8, :] = jnp.zeros((8, D_A), F32)
        vprev_ref[...] = jnp.zeros((STEP, D_B), BF16)

    @pl.when(i >= NP_STEPS)
    def _():
        ctx_ref[6:8, :] = st_ref[...]

    ctx_ref[8:8 + STEP, :] = z
    conv = (wc_ref[0:1, :] * ctx_ref[6:6 + STEP, :]
            + wc_ref[1:2, :] * ctx_ref[7:7 + STEP, :]
            + wc_ref[2:3, :] * z)
    y_ref[:, 0:D_A] = (p_ref[:, D_A:2 * D_A] * conv).astype(y_ref.dtype)
    c3_ref[...] = ctx_ref[6 + STEP:8 + STEP, :]
    ctx_ref[0:8, :] = ctx_ref[STEP:STEP + 8, :]

    u = jax.nn.gelu(p_ref[:, 3 * D_A:3 * D_A + D_B])
    v = _ln(jax.nn.gelu(p_ref[:, 3 * D_A + D_B:3 * D_A + 2 * D_B]), g_ref[...], b_ref[...])
    vrow_ref[...] = v
    vb = v.astype(BF16)
    second_chunk = jnp.logical_and(i < NP_STEPS, i % 2 == 1)
    row = lax.broadcasted_iota(jnp.int32, (2 * STEP, 2 * STEP), 0)
    col = lax.broadcasted_iota(jnp.int32, (2 * STEP, 2 * STEP), 1)
    mask = (col // STEP) <= (row // STEP)
    for h in range(SGU_HEADS):
        hs = slice(h * 128, (h + 1) * 128)
        wm = jnp.where(mask, ws_ref[h], 0.0).astype(BF16)
        v_h = vb[:, hs]
        f_first = jnp.dot(wm[0:STEP, 0:STEP], v_h, preferred_element_type=F32) + bs_ref[h, 0:STEP, :]
        f_second = (jnp.dot(wm[STEP:, 0:STEP], vprev_ref[:, hs], preferred_element_type=F32)
                    + jnp.dot(wm[STEP:, STEP:], v_h, preferred_element_type=F32)
                    + bs_ref[h, STEP:, :])
        f = jnp.where(second_chunk, f_second, f_first)
        y_ref[:, D_A + h * 128:D_A + (h + 1) * 128] = (u[:, hs] * f).astype(y_ref.dtype)
    vprev_ref[...] = vb


def _mix0(proj, state_c3, w_conv3, g, b, w_spatial, b_spatial_col):
    n_steps = NP_STEPS + N_STREAMS
    return pl.pallas_call(
        _mix0_body,
        grid=(n_steps,),
        in_specs=[pl.BlockSpec((STEP, 3 * D_A + 2 * D_B), lambda i: (i, 0)),
                  pl.BlockSpec((None, 2, D_A), lambda i: (_state_block(i), 0, 0)),
                  pl.BlockSpec((3, D_A), lambda i: (0, 0)),
                  pl.BlockSpec((1, D_B), lambda i: (0, 0)),
                  pl.BlockSpec((1, D_B), lambda i: (0, 0)),
                  pl.BlockSpec((SGU_HEADS, 128, 128), lambda i: (0, 0, 0)),
                  pl.BlockSpec((SGU_HEADS, 128, 1), lambda i: (0, 0, 0))],
        out_specs=[pl.BlockSpec((STEP, D), lambda i: (i, 0)),
                   pl.BlockSpec((None, 2, D_A), lambda i: (_new_state_block(i), 0, 0)),
                   pl.BlockSpec((None, STEP, D_B), lambda i: (_state_block(i), 0, 0))],
        out_shape=[jax.ShapeDtypeStruct((M_TOK, D), BF16),
                   jax.ShapeDtypeStruct((N_STREAMS + 1, 2, D_A), F32),
                   jax.ShapeDtypeStruct((N_STREAMS, STEP, D_B), F32)],
        scratch_shapes=[pltpu.VMEM((STEP + 8, D_A), F32),
                        pltpu.VMEM((STEP, D_B), BF16)],
        compiler_params=_cparams(1),
        name="mix_even",
    )(proj, state_c3, w_conv3, g, b, w_spatial, b_spatial_col)


CONF_W = 31
CTX_PAD = 32
CONV_COLS = 512


def _mix1_body(p_ref, st_ref, w_ref, bdw_ref, g_ref, b_ref, y_ref, c31_ref, ctx_ref, cv_ref):
    i = pl.program_id(0)
    glu = p_ref[:, 0:D] * jax.nn.sigmoid(p_ref[:, D:2 * D])

    @pl.when(i == 0)
    def _():
        ctx_ref[0:CTX_PAD, :] = jnp.zeros((CTX_PAD, D), F32)

    @pl.when(i >= NP_STEPS)
    def _():
        ctx_ref[CTX_PAD - (CONF_W - 1):CTX_PAD, :] = st_ref[...]

    ctx_ref[CTX_PAD:CTX_PAD + STEP, :] = glu
    c31_ref[...] = ctx_ref[CTX_PAD + STEP - (CONF_W - 1):CTX_PAD + STEP, :]
    first = CTX_PAD - (CONF_W - 1)
    for c in range(D // CONV_COLS):
        cs = slice(c * CONV_COLS, (c + 1) * CONV_COLS)
        acc = w_ref[0:1, cs] * ctx_ref[first:first + STEP, cs]
        for k in range(1, CONF_W):
            acc = acc + w_ref[k:k + 1, cs] * ctx_ref[first + k:first + k + STEP, cs]
        cv_ref[:, cs] = acc + bdw_ref[:, cs]
    c = _ln(cv_ref[...], g_ref[...], b_ref[...])
    y_ref[...] = (c * jax.nn.sigmoid(c)).astype(y_ref.dtype)
    ctx_ref[0:CTX_PAD, :] = ctx_ref[STEP:STEP + CTX_PAD, :]


def _mix1(proj, state_c31, w_dw, b_dw, g, b):
    n_steps = NP_STEPS + N_STREAMS
    return pl.pallas_call(
        _mix1_body,
        grid=(n_steps,),
        in_specs=[pl.BlockSpec((STEP, 2 * D), lambda i: (i, 0)),
                  pl.BlockSpec((None, CONF_W - 1, D), lambda i: (_state_block(i), 0, 0)),
                  pl.BlockSpec((CONF_W, D), lambda i: (0, 0)),
                  pl.BlockSpec((1, D), lambda i: (0, 0)),
                  pl.BlockSpec((1, D), lambda i: (0, 0)),
                  pl.BlockSpec((1, D), lambda i: (0, 0))],
        out_specs=[pl.BlockSpec((STEP, D), lambda i: (i, 0)),
                   pl.BlockSpec((None, CONF_W - 1, D), lambda i: (_new_state_block(i), 0, 0))],
        out_shape=[jax.ShapeDtypeStruct((M_TOK, D), BF16),
                   jax.ShapeDtypeStruct((N_STREAMS + 1, CONF_W - 1, D), F32)],
        scratch_shapes=[pltpu.VMEM((CTX_PAD + STEP, D), F32),
                        pltpu.VMEM((STEP, D), F32)],
        compiler_params=_cparams(1),
        name="mix_odd",
    )(proj, state_c31, w_dw, b_dw, g, b)


def _attn_body(q_ref, k_ref, v_ref, *rest):
    o_ref = rest[-1]
    for h in range(HEADS):
        hs = slice(h * HEAD_DIM, (h + 1) * HEAD_DIM)
        s = lax.dot_general(q_ref[:, hs], k_ref[:, hs].astype(BF16),
                            (((1,), (1,)), ((), ())), preferred_element_type=F32) * ATTN_SCALE
        p = jnp.exp(s - jnp.max(s, axis=-1, keepdims=True))
        p = p / jnp.sum(p, axis=-1, keepdims=True)
        o_ref[:, hs] = _bdot(p, v_ref[:, hs]).astype(o_ref.dtype)


def _attn(q, pk, pv, ck, cv, layer):
    tq = 512
    o = pl.pallas_call(
        _attn_body,
        grid=(N_PROMPT // tq,),
        in_specs=[pl.BlockSpec((tq, D), lambda i: (i, 0)),
                  pl.BlockSpec((None, N_MEM, D), lambda i: (layer, 0, 0)),
                  pl.BlockSpec((None, N_MEM, D), lambda i: (layer, 0, 0))],
        out_specs=pl.BlockSpec((tq, D), lambda i: (i, 0)),
        out_shape=jax.ShapeDtypeStruct((M_TOK, D), BF16),
        compiler_params=_cparams(1),
        name="attn_prompt",
    )(q, pk, pv)
    return pl.pallas_call(
        _attn_body,
        grid=(N_STREAMS,),
        in_specs=[pl.BlockSpec((STEP, D), lambda n: (NP_STEPS + n, 0)),
                  pl.BlockSpec((None, None, N_MEM, D), lambda n: (layer, n, 0, 0)),
                  pl.BlockSpec((None, None, N_MEM, D), lambda n: (layer, n, 0, 0)),
                  pl.BlockSpec(memory_space=pl.ANY)],
        out_specs=pl.BlockSpec((STEP, D), lambda n: (NP_STEPS + n, 0)),
        out_shape=jax.ShapeDtypeStruct((M_TOK, D), BF16),
        input_output_aliases={3: 0},
        compiler_params=_cparams(1),
        name="attn_sample",
    )(q, ck, cv, o)


def _split_bf16(x):
    hi = x.astype(BF16)
    lo = (x - hi.astype(F32)).astype(BF16)
    return hi, lo


def _route(x, wr, br):
    xh, xl = _split_bf16(x)
    wh, wl = _split_bf16(wr)
    logits = (jnp.dot(xh, wh, preferred_element_type=F32)
              + jnp.dot(xl, wh, preferred_element_type=F32)
              + jnp.dot(xh, wl, preferred_element_type=F32)) + br
    lane = lax.broadcasted_iota(jnp.int32, logits.shape, 1)

    def first_argmax(vals):
        top = jnp.max(vals, axis=-1, keepdims=True)
        idx = jnp.min(jnp.where(vals == top, lane, ROUTE_LANES), axis=-1, keepdims=True)
        return top, idx

    g_logits = jnp.where(lane < N_GROUPS, logits, NEG)
    g_top, g_idx = first_argmax(g_logits)
    p_group = 1.0 / jnp.sum(jnp.exp(g_logits - g_top), axis=-1, keepdims=True)
    lo_lane = N_GROUPS + E_PER * g_idx
    e_logits = jnp.where(jnp.logical_and(lane >= lo_lane, lane < lo_lane + E_PER), logits, NEG)
    v1, i1 = first_argmax(e_logits)
    v2, i2 = first_argmax(jnp.where(lane == i1, NEG, e_logits))
    t = jnp.exp(v2 - v1)
    w1 = p_group / (1.0 + t)
    w2 = p_group * t / (1.0 + t)
    return jnp.where(lane == i1, w1, 0.0) + jnp.where(lane == i2, w2, 0.0)


def _moe_dense_body(x_ref, wr_ref, br_ref, wg_ref, wu_ref, wd_ref, g_ref, b_ref,
                    o_ref, xb_ref, comb_ref, acc_ref):
    e = pl.program_id(1)

    @pl.when(e == 0)
    def _():
        x = x_ref[...]
        xb_ref[...] = x.astype(BF16)
        comb_ref[...] = _route(x, wr_ref[...], br_ref[...])
        acc_ref[...] = jnp.zeros_like(acc_ref)

    lane = lax.broadcasted_iota(jnp.int32, comb_ref.shape, 1)
    c_e = jnp.sum(jnp.where(lane == N_GROUPS + e, comb_ref[...], 0.0), axis=-1, keepdims=True)
    xb = xb_ref[...]
    gate = jnp.dot(xb, wg_ref[...].astype(BF16), preferred_element_type=F32)
    up = jnp.dot(xb, wu_ref[...].astype(BF16), preferred_element_type=F32)
    h = gate * jax.nn.sigmoid(gate) * up
    acc_ref[...] += _bdot(h * c_e, wd_ref[...])

    @pl.when(e == N_EXPERTS - 1)
    def _():
        o_ref[...] = _ln(ALPHA * x_ref[...] + acc_ref[...], g_ref[...], b_ref[...])


def _moe_dense(x, wr, br, w_gate, w_up, w_down, layer, g, b, tm=512):
    m = x.shape[0]
    return pl.pallas_call(
        _moe_dense_body,
        grid=(m // tm, N_EXPERTS),
        in_specs=[pl.BlockSpec((tm, D), lambda i, e: (i, 0)),
                  pl.BlockSpec((D, ROUTE_LANES), lambda i, e: (0, 0)),
                  pl.BlockSpec((1, ROUTE_LANES), lambda i, e: (0, 0)),
                  pl.BlockSpec((None, None, D, D_EXPERT), lambda i, e: (layer, e, 0, 0)),
                  pl.BlockSpec((None, None, D, D_EXPERT), lambda i, e: (layer, e, 0, 0)),
                  pl.BlockSpec((None, None, D_EXPERT, D), lambda i, e: (layer, e, 0, 0)),
                  pl.BlockSpec((1, D), lambda i, e: (0, 0)),
                  pl.BlockSpec((1, D), lambda i, e: (0, 0))],
        out_specs=pl.BlockSpec((tm, D), lambda i, e: (i, 0)),
        out_shape=jax.ShapeDtypeStruct((m, D), F32),
        scratch_shapes=[pltpu.VMEM((tm, D), BF16),
                        pltpu.VMEM((tm, ROUTE_LANES), F32),
                        pltpu.VMEM((tm, D), F32)],
        compiler_params=_cparams(2),
        name="moe_dense",
    )(x, wr, br, w_gate, w_up, w_down, g, b)


def _router_params(w_rg, b_rg, w_re, b_re):
    w = jnp.concatenate([w_rg, jnp.transpose(w_re, (1, 0, 2)).reshape(D, N_EXPERTS)], axis=1)
    bias = jnp.concatenate([b_rg, b_re.reshape(N_EXPERTS)])
    pad = ROUTE_LANES - w.shape[1]
    return jnp.pad(w, ((0, 0), (0, pad))), jnp.pad(bias, (0, pad))[None, :]


def kernel(x_prompt, x_sample, mem_prompt, cache_mem_k, cache_mem_v, state_conv3, state_conv31,
           ln_gain, ln_bias, w_in_ab, w_conv3, ln_sgu_gain, ln_sgu_bias, w_spatial, b_spatial, w_out_ab,
           w_pw1, w_dw31, b_dw31, ln_conf_gain, ln_conf_bias, w_pw2,
           w_mem_q, w_mem_k, w_mem_v, w_mem_o,
           w_route_group, b_route_group, w_route_expert, b_route_expert,
           w_exp_gate, w_exp_up, w_exp_down):
    depth = ln_gain.shape[0]
    x = jnp.concatenate([x_prompt.reshape(N_PROMPT, D), x_sample.reshape(N_STREAMS * STEP, D)], axis=0)

    mem = mem_prompt.reshape(N_MEM, D)
    pk = _mem_kv(mem, w_mem_k, "mem_k")
    pv = _mem_kv(mem, w_mem_v, "mem_v")
    ck = cache_mem_k.reshape(depth, N_STREAMS, N_MEM, D)
    cv = cache_mem_v.reshape(depth, N_STREAMS, N_MEM, D)

    def ln_params(l, k):
        return ln_gain[l, k][None, :], ln_bias[l, k][None, :]

    c3_new = c31_new = v_rows = None
    for l in range(depth):
        i = l // 2
        if l % 2 == 0:
            proj = _mm(x, w_in_ab, i, F32, 1024, 512, "proj_in_ab")
            y_mix, c3_new, v_rows = _mix0(
                proj, state_conv3[i], w_conv3[i], ln_sgu_gain[i][None, :], ln_sgu_bias[i][None, :],
                w_spatial[i], b_spatial[i][:, :, None])
            w_out = w_out_ab
        else:
            proj = _mm(x, w_pw1, i, F32, 1024, 512, "proj_pw1")
            y_mix, c31_new = _mix1(
                proj, state_conv31[i], w_dw31[i], b_dw31[i][None, :],
                ln_conf_gain[i][None, :], ln_conf_bias[i][None, :])
            w_out = w_pw2
        x = _mm_ln(y_mix, w_out, i, x, *ln_params(l, 0), name="mix_out_ln")
        q = _mm(x, w_mem_q, l, BF16, 1024, 512, "mem_q")
        o = _attn(q, pk, pv, ck, cv, l)
        x = _mm_ln(o, w_mem_o, l, x, *ln_params(l, 1), name="mem_out_ln")
        wr, br = _router_params(w_route_group[l], b_route_group[l], w_route_expert[l], b_route_expert[l])
        x = _moe_dense(x, wr, br, w_exp_gate, w_exp_up, w_exp_down, l, *ln_params(l, 2))

    y_prompt = x[:N_PROMPT].reshape(1, N_PROMPT, D)
    y_sample = x[N_PROMPT:].reshape(N_STREAMS, STEP, D)
    prompt_mem_k = pk.reshape(depth, 1, N_MEM, HEADS, HEAD_DIM)
    prompt_mem_v = pv.reshape(depth, 1, N_MEM, HEADS, HEAD_DIM)
    prompt_conv3 = c3_new[0].reshape(1, 1, 2, D_A)
    sample_conv3 = c3_new[1:].reshape(1, N_STREAMS, 2, D_A)
    prompt_conv31 = c31_new[0].reshape(1, 1, CONF_W - 1, D)
    sample_conv31 = c31_new[1:].reshape(1, N_STREAMS, CONF_W - 1, D)
    sample_sgu_v = v_rows.reshape(1, N_STREAMS, STEP, D_B)
    return (y_prompt, y_sample, prompt_mem_k, prompt_mem_v, prompt_conv3, prompt_conv31,
            sample_conv3, sample_conv31, sample_sgu_v)
```

```python
import functools

import jax
import jax.numpy as jnp
from jax import lax
from jax.experimental import pallas as pl
from jax.experimental.pallas import tpu as pltpu

F32 = jnp.float32
BF16 = jnp.bfloat16

D = 2048
N_PROMPT = 8192
N_STREAMS = 16
STEP = 64
NP_STEPS = N_PROMPT // STEP
M_TOK = N_PROMPT + N_STREAMS * STEP
N_MEM = 256
HEADS = 4
HEAD_DIM = D // HEADS
D_A = 1024
D_B = 1024
SGU_HEADS = 8
N_GROUPS = 4
E_PER = 4
N_EXPERTS = 16
D_EXPERT = 512
LN_EPS = 1e-5
ALPHA = (2.0 * 2) ** 0.25
ATTN_SCALE = HEAD_DIM ** -0.5
ROUTE_LANES = 128
NEG = -1e30

VMEM_LIMIT_BYTES = 60000 * 1024


def _cparams(n_axes):
    return pltpu.CompilerParams(
        dimension_semantics=("arbitrary",) * n_axes,
        vmem_limit_bytes=VMEM_LIMIT_BYTES)


def _ln(x, g, b):
    mu = jnp.mean(x, axis=-1, keepdims=True)
    xc = x - mu
    var = jnp.mean(xc * xc, axis=-1, keepdims=True)
    return xc * lax.rsqrt(var + LN_EPS) * g + b


def _bdot(a, b):
    return jnp.dot(a.astype(BF16), b.astype(BF16), preferred_element_type=F32)


def _conv_operand(x):
    return x.astype(BF16).astype(F32)


def _mm_body(x_ref, w_ref, o_ref):
    o_ref[...] = _bdot(x_ref[...], w_ref[...]).astype(o_ref.dtype)


def _mm(x, w, layer, out_dtype, tm, tn, name):
    m, k = x.shape
    n = w.shape[-1]
    return pl.pallas_call(
        _mm_body,
        grid=(m // tm, n // tn),
        in_specs=[pl.BlockSpec((tm, k), lambda i, j: (i, 0)),
                  pl.BlockSpec((None, k, tn), lambda i, j: (layer, 0, j))],
        out_specs=pl.BlockSpec((tm, tn), lambda i, j: (i, j)),
        out_shape=jax.ShapeDtypeStruct((m, n), out_dtype),
        compiler_params=_cparams(2),
        name=name,
    )(x, w)


def _mem_kv(mem, w, name):
    depth, k, n = w.shape
    tn = 512
    return pl.pallas_call(
        _mm_body,
        grid=(depth, n // tn),
        in_specs=[pl.BlockSpec((N_MEM, k), lambda l, j: (0, 0)),
                  pl.BlockSpec((None, k, tn), lambda l, j: (l, 0, j))],
        out_specs=pl.BlockSpec((None, N_MEM, tn), lambda l, j: (l, 0, j)),
        out_shape=jax.ShapeDtypeStruct((depth, N_MEM, n), F32),
        compiler_params=_cparams(2),
        name=name,
    )(mem, w)


def _mm_ln_body(x_ref, w_ref, r_ref, g_ref, b_ref, o_ref, acc_ref, *, nj, tn):
    j = pl.program_id(1)
    acc_ref[j] = _bdot(x_ref[...], w_ref[...])

    @pl.when(j == nj - 1)
    def _():
        parts = [ALPHA * r_ref[:, k * tn:(k + 1) * tn] + acc_ref[k] for k in range(nj)]
        y = jnp.concatenate(parts, axis=-1)
        o_ref[...] = _ln(y, g_ref[...], b_ref[...])


def _mm_ln(x, w, layer, res, g, b, name, tm=512, tn=512):
    m, k = x.shape
    n = w.shape[-1]
    nj = n // tn
    return pl.pallas_call(
        functools.partial(_mm_ln_body, nj=nj, tn=tn),
        grid=(m // tm, nj),
        in_specs=[pl.BlockSpec((tm, k), lambda i, j: (i, 0)),
                  pl.BlockSpec((None, k, tn), lambda i, j: (layer, 0, j)),
                  pl.BlockSpec((tm, n), lambda i, j: (i, 0)),
                  pl.BlockSpec((1, n), lambda i, j: (0, 0)),
                  pl.BlockSpec((1, n), lambda i, j: (0, 0))],
        out_specs=pl.BlockSpec((tm, n), lambda i, j: (i, 0)),
        out_shape=jax.ShapeDtypeStruct((m, n), F32),
        scratch_shapes=[pltpu.VMEM((nj, tm, tn), F32)],
        compiler_params=_cparams(2),
        name=name,
    )(x, w, res, g, b)


def _state_block(i):
    return jnp.maximum(i - NP_STEPS, 0)


def _new_state_block(i):
    return jnp.where(i < NP_STEPS, 0, i - NP_STEPS + 1)


def _mix0_body(p_ref, st_ref, wc_ref, g_ref, b_ref, ws_ref, bs_ref,
               y_ref, c3_ref, vrow_ref, ctx_ref, vprev_ref):
    i = pl.program_id(0)

    z = p_ref[:, 2 * D_A:3 * D_A] * p_ref[:, 0:D_A]

    @pl.when(i == 0)
    def _():
        ctx_ref[0:8, :] = jnp.zeros((8, D_A), F32)
        vprev_ref[...] = jnp.zeros((STEP, D_B), BF16)

    @pl.when(i >= NP_STEPS)
    def _():
        ctx_ref[6:8, :] = _conv_operand(st_ref[...])

    zc = _conv_operand(z)
    ctx_ref[8:8 + STEP, :] = zc
    conv = (wc_ref[0:1, :] * ctx_ref[6:6 + STEP, :]
            + wc_ref[1:2, :] * ctx_ref[7:7 + STEP, :]
            + wc_ref[2:3, :] * zc)
    y_ref[:, 0:D_A] = (p_ref[:, D_A:2 * D_A] * conv).astype(y_ref.dtype)
    c3_ref[...] = z[STEP - 2:STEP, :]
    ctx_ref[0:8, :] = ctx_ref[STEP:STEP + 8, :]

    u = jax.nn.gelu(p_ref[:, 3 * D_A:3 * D_A + D_B])
    v = _ln(jax.nn.gelu(p_ref[:, 3 * D_A + D_B:3 * D_A + 2 * D_B]), g_ref[...], b_ref[...])
    vrow_ref[...] = v
    vb = v.astype(BF16)
    second_chunk = jnp.logical_and(i < NP_STEPS, i % 2 == 1)
    row = lax.broadcasted_iota(jnp.int32, (2 * STEP, 2 * STEP), 0)
    col = lax.broadcasted_iota(jnp.int32, (2 * STEP, 2 * STEP), 1)
    mask = (col // STEP) <= (row // STEP)
    for h in range(SGU_HEADS):
        hs = slice(h * 128, (h + 1) * 128)
        wm = jnp.where(mask, ws_ref[h], 0.0).astype(BF16)
        v_h = vb[:, hs]
        f_first = jnp.dot(wm[0:STEP, 0:STEP], v_h, preferred_element_type=F32) + bs_ref[h, 0:STEP, :]
        f_second = (jnp.dot(wm[STEP:, 0:STEP], vprev_ref[:, hs], preferred_element_type=F32)
                    + jnp.dot(wm[STEP:, STEP:], v_h, preferred_element_type=F32)
                    + bs_ref[h, STEP:, :])
        f = jnp.where(second_chunk, f_second, f_first)
        y_ref[:, D_A + h * 128:D_A + (h + 1) * 128] = (u[:, hs] * f).astype(y_ref.dtype)
    vprev_ref[...] = vb


def _mix0(proj, state_c3, w_conv3, g, b, w_spatial, b_spatial_col):
    n_steps = NP_STEPS + N_STREAMS
    return pl.pallas_call(
        _mix0_body,
        grid=(n_steps,),
        in_specs=[pl.BlockSpec((STEP, 3 * D_A + 2 * D_B), lambda i: (i, 0)),
                  pl.BlockSpec((None, 2, D_A), lambda i: (_state_block(i), 0, 0)),
                  pl.BlockSpec((3, D_A), lambda i: (0, 0)),
                  pl.BlockSpec((1, D_B), lambda i: (0, 0)),
                  pl.BlockSpec((1, D_B), lambda i: (0, 0)),
                  pl.BlockSpec((SGU_HEADS, 128, 128), lambda i: (0, 0, 0)),
                  pl.BlockSpec((SGU_HEADS, 128, 1), lambda i: (0, 0, 0))],
        out_specs=[pl.BlockSpec((STEP, D), lambda i: (i, 0)),
                   pl.BlockSpec((None, 2, D_A), lambda i: (_new_state_block(i), 0, 0)),
                   pl.BlockSpec((None, STEP, D_B), lambda i: (_state_block(i), 0, 0))],
        out_shape=[jax.ShapeDtypeStruct((M_TOK, D), BF16),
                   jax.ShapeDtypeStruct((N_STREAMS + 1, 2, D_A), F32),
                   jax.ShapeDtypeStruct((N_STREAMS, STEP, D_B), F32)],
        scratch_shapes=[pltpu.VMEM((STEP + 8, D_A), F32),
                        pltpu.VMEM((STEP, D_B), BF16)],
        compiler_params=_cparams(1),
        name="mix_even",
    )(proj, state_c3, w_conv3, g, b, w_spatial, b_spatial_col)


CONF_W = 31
CTX_PAD = 32
CONV_COLS = 512


def _mix1_body(p_ref, st_ref, w_ref, bdw_ref, g_ref, b_ref, y_ref, c31_ref, ctx_ref, cv_ref):
    i = pl.program_id(0)
    glu = p_ref[:, 0:D] * jax.nn.sigmoid(p_ref[:, D:2 * D])

    @pl.when(i == 0)
    def _():
        ctx_ref[0:CTX_PAD, :] = jnp.zeros((CTX_PAD, D), F32)

    @pl.when(i >= NP_STEPS)
    def _():
        ctx_ref[CTX_PAD - (CONF_W - 1):CTX_PAD, :] = _conv_operand(st_ref[...])

    ctx_ref[CTX_PAD:CTX_PAD + STEP, :] = _conv_operand(glu)
    c31_ref[...] = glu[STEP - (CONF_W - 1):STEP, :]
    first = CTX_PAD - (CONF_W - 1)
    for c in range(D // CONV_COLS):
        cs = slice(c * CONV_COLS, (c + 1) * CONV_COLS)
        acc = w_ref[0:1, cs] * ctx_ref[first:first + STEP, cs]
        for k in range(1, CONF_W):
            acc = acc + w_ref[k:k + 1, cs] * ctx_ref[first + k:first + k + STEP, cs]
        cv_ref[:, cs] = acc + bdw_ref[:, cs]
    c = _ln(cv_ref[...], g_ref[...], b_ref[...])
    y_ref[...] = (c * jax.nn.sigmoid(c)).astype(y_ref.dtype)
    ctx_ref[0:CTX_PAD, :] = ctx_ref[STEP:STEP + CTX_PAD, :]


def _mix1(proj, state_c31, w_dw, b_dw, g, b):
    n_steps = NP_STEPS + N_STREAMS
    return pl.pallas_call(
        _mix1_body,
        grid=(n_steps,),
        in_specs=[pl.BlockSpec((STEP, 2 * D), lambda i: (i, 0)),
                  pl.BlockSpec((None, CONF_W - 1, D), lambda i: (_state_block(i), 0, 0)),
                  pl.BlockSpec((CONF_W, D), lambda i: (0, 0)),
                  pl.BlockSpec((1, D), lambda i: (0, 0)),
                  pl.BlockSpec((1, D), lambda i: (0, 0)),
                  pl.BlockSpec((1, D), lambda i: (0, 0))],
        out_specs=[pl.BlockSpec((STEP, D), lambda i: (i, 0)),
                   pl.BlockSpec((None, CONF_W - 1, D), lambda i: (_new_state_block(i), 0, 0))],
        out_shape=[jax.ShapeDtypeStruct((M_TOK, D), BF16),
                   jax.ShapeDtypeStruct((N_STREAMS + 1, CONF_W - 1, D), F32)],
        scratch_shapes=[pltpu.VMEM((CTX_PAD + STEP, D), F32),
                        pltpu.VMEM((STEP, D), F32)],
        compiler_params=_cparams(1),
        name="mix_odd",
    )(proj, state_c31, w_dw, b_dw, g, b)


def _attn_body(q_ref, k_ref, v_ref, *rest):
    o_ref = rest[-1]
    for h in range(HEADS):
        hs = slice(h * HEAD_DIM, (h + 1) * HEAD_DIM)
        s = lax.dot_general(q_ref[:, hs], k_ref[:, hs].astype(BF16),
                            (((1,), (1,)), ((), ())), preferred_element_type=F32) * ATTN_SCALE
        p = jnp.exp(s - jnp.max(s, axis=-1, keepdims=True))
        p = p / jnp.sum(p, axis=-1, keepdims=True)
        o_ref[:, hs] = _bdot(p, v_ref[:, hs]).astype(o_ref.dtype)


def _attn_prompt_body(q_ref, k_ref, v_ref, o_ref, *, n_prompt_steps):
    @pl.when(pl.program_id(0) < n_prompt_steps)
    def _():
        _attn_body(q_ref, k_ref, v_ref, o_ref)

    @pl.when(pl.program_id(0) >= n_prompt_steps)
    def _():
        o_ref[...] = jnp.zeros_like(o_ref)


def _attn(q, pk, pv, ck, cv, layer):
    tq = 512
    o = pl.pallas_call(
        functools.partial(_attn_prompt_body, n_prompt_steps=N_PROMPT // tq),
        grid=(M_TOK // tq,),
        in_specs=[pl.BlockSpec((tq, D), lambda i: (i, 0)),
                  pl.BlockSpec((None, N_MEM, D), lambda i: (layer, 0, 0)),
                  pl.BlockSpec((None, N_MEM, D), lambda i: (layer, 0, 0))],
        out_specs=pl.BlockSpec((tq, D), lambda i: (i, 0)),
        out_shape=jax.ShapeDtypeStruct((M_TOK, D), BF16),
        compiler_params=_cparams(1),
        name="attn_prompt",
    )(q, pk, pv)
    return pl.pallas_call(
        _attn_body,
        grid=(N_STREAMS,),
        in_specs=[pl.BlockSpec((STEP, D), lambda n: (NP_STEPS + n, 0)),
                  pl.BlockSpec((None, None, N_MEM, D), lambda n: (layer, n, 0, 0)),
                  pl.BlockSpec((None, None, N_MEM, D), lambda n: (layer, n, 0, 0)),
                  pl.BlockSpec(memory_space=pl.ANY)],
        out_specs=pl.BlockSpec((STEP, D), lambda n: (NP_STEPS + n, 0)),
        out_shape=jax.ShapeDtypeStruct((M_TOK, D), BF16),
        input_output_aliases={3: 0},
        compiler_params=_cparams(1),
        name="attn_sample",
    )(q, ck, cv, o)


TOP_K = 2
ROUTE_TM = 512
EXP_TM = 256
N_ASSIGN = M_TOK * TOP_K
SORTED_ROWS = N_ASSIGN + N_EXPERTS * EXP_TM
N_EXP_TILES = SORTED_ROWS // EXP_TM
COMB_TM = 256


def _route_body(x_ref, wr_ref, br_ref, idx_ref, wts_ref, cnt_ref, run_ref):
    i = pl.program_id(0)

    @pl.when(i == 0)
    def _():
        run_ref[...] = jnp.zeros_like(run_ref)

    logits = _bdot(x_ref[...], wr_ref[...]) + br_ref[...]
    lane = lax.broadcasted_iota(jnp.int32, logits.shape, 1)

    def first_argmax(vals):
        top = jnp.max(vals, axis=-1, keepdims=True)
        idx = jnp.min(jnp.where(vals == top, lane, ROUTE_LANES), axis=-1, keepdims=True)
        return top, idx

    g_logits = jnp.where(lane < N_GROUPS, logits, NEG)
    g_top, g_idx = first_argmax(g_logits)
    p_group = 1.0 / jnp.sum(jnp.exp(g_logits - g_top), axis=-1, keepdims=True)
    lo_lane = N_GROUPS + E_PER * g_idx
    e_logits = jnp.where(jnp.logical_and(lane >= lo_lane, lane < lo_lane + E_PER), logits, NEG)
    v1, i1 = first_argmax(e_logits)
    v2, i2 = first_argmax(jnp.where(lane == i1, NEG, e_logits))
    t = jnp.exp(v2 - v1)
    w1 = p_group / (1.0 + t)
    w2 = p_group * t / (1.0 + t)

    a1 = (lane == i1).astype(F32)
    a2 = (lane == i2).astype(F32)
    a = a1 + a2
    tm = logits.shape[0]
    earlier = (lax.broadcasted_iota(jnp.int32, (tm, tm), 1)
               < lax.broadcasted_iota(jnp.int32, (tm, tm), 0)).astype(BF16)
    before = jnp.dot(earlier, a.astype(BF16), preferred_element_type=F32) + run_ref[...]
    r1 = jnp.sum(a1 * before, axis=-1, keepdims=True)
    r2 = jnp.sum(a2 * before, axis=-1, keepdims=True)
    run_ref[...] += jnp.sum(a, axis=0, keepdims=True)
    cnt_ref[...] = run_ref[...].astype(jnp.int32)

    e1 = (i1 - N_GROUPS).astype(F32)
    e2 = (i2 - N_GROUPS).astype(F32)
    packed = (jnp.where(lane == 0, e1, 0.0) + jnp.where(lane == 1, e2, 0.0)
              + jnp.where(lane == 2, r1, 0.0) + jnp.where(lane == 3, r2, 0.0))
    idx_ref[...] = packed.T[0:8, :].astype(jnp.int32)
    wts_ref[...] = jnp.where(lane == 0, w1, 0.0) + jnp.where(lane == 1, w2, 0.0)


def _route(x, wr, br):
    m = x.shape[0]
    tm = ROUTE_TM
    return pl.pallas_call(
        _route_body,
        grid=(m // tm,),
        in_specs=[pl.BlockSpec((tm, D), lambda i: (i, 0)),
                  pl.BlockSpec((D, ROUTE_LANES), lambda i: (0, 0)),
                  pl.BlockSpec((1, ROUTE_LANES), lambda i: (0, 0))],
        out_specs=[pl.BlockSpec((8, tm), lambda i: (0, i)),
                   pl.BlockSpec((tm, ROUTE_LANES), lambda i: (i, 0)),
                   pl.BlockSpec((1, ROUTE_LANES), lambda i: (0, 0))],
        out_shape=[jax.ShapeDtypeStruct((8, m), jnp.int32),
                   jax.ShapeDtypeStruct((m, ROUTE_LANES), F32),
                   jax.ShapeDtypeStruct((1, ROUTE_LANES), jnp.int32)],
        scratch_shapes=[pltpu.VMEM((1, ROUTE_LANES), F32)],
        compiler_params=_cparams(1),
        name="moe_route",
    )(x, wr, br)


def _plan_body(e1_ref, e2_ref, r1_ref, r2_ref, cnt_ref,
               src_ref, pos1_ref, pos2_ref, tile_e_ref, nact_ref, off_ref):
    shift = EXP_TM.bit_length() - 1
    off_tiles = jnp.int32(0)
    for e in range(N_EXPERTS):
        off_ref[e] = off_tiles * EXP_TM
        n_tiles = lax.shift_right_logical(cnt_ref[N_GROUPS + e] + (EXP_TM - 1), shift)

        def mark(k, carry, e=e, base=off_tiles):
            tile_e_ref[base + k] = e
            return carry

        lax.fori_loop(0, n_tiles, mark, 0)
        off_tiles = off_tiles + n_tiles
    nact_ref[0] = off_tiles

    def mark_idle(k, carry):
        tile_e_ref[k] = N_EXPERTS - 1
        return carry

    lax.fori_loop(off_tiles, N_EXP_TILES, mark_idle, 0)

    def clear(p, carry):
        src_ref[p] = 0
        return carry

    lax.fori_loop(0, SORTED_ROWS, clear, 0)

    def place(t, carry):
        p1 = off_ref[e1_ref[t]] + r1_ref[t]
        p2 = off_ref[e2_ref[t]] + r2_ref[t]
        src_ref[p1] = t
        src_ref[p2] = t
        pos1_ref[t] = p1
        pos2_ref[t] = p2
        return carry

    lax.fori_loop(0, M_TOK, place, 0)


def _plan(e1, e2, r1, r2, cnt):
    smem = pl.BlockSpec(memory_space=pltpu.SMEM)
    return pl.pallas_call(
        _plan_body,
        in_specs=[smem] * 5,
        out_specs=[smem] * 5,
        out_shape=[jax.ShapeDtypeStruct((SORTED_ROWS,), jnp.int32),
                   jax.ShapeDtypeStruct((M_TOK,), jnp.int32),
                   jax.ShapeDtypeStruct((M_TOK,), jnp.int32),
                   jax.ShapeDtypeStruct((N_EXP_TILES,), jnp.int32),
                   jax.ShapeDtypeStruct((1,), jnp.int32)],
        scratch_shapes=[pltpu.SMEM((N_EXPERTS,), jnp.int32)],
        name="moe_plan",
    )(e1, e2, r1, r2, cnt)


def _experts_body(tile_e_ref, nact_ref, src_ref, x_hbm, wg_ref, wu_ref, wd_ref,
                  o_ref, xbuf, sem, wg_b, wu_b, wd_b):
    j = pl.program_id(0)
    nact = nact_ref[0]
    slot = j % 2

    def gather_rows(tile, s):
        def body(r, carry):
            tok = src_ref[tile * EXP_TM + r]
            pltpu.make_async_copy(x_hbm.at[pl.ds(tok, 1)], xbuf.at[s, pl.ds(r, 1)], sem.at[s]).start()
            return carry
        lax.fori_loop(0, EXP_TM, body, 0)

    @pl.when(j == 0)
    def _():
        gather_rows(0, 0)

    @pl.when(j + 1 < nact)
    def _():
        gather_rows(j + 1, 1 - slot)

    @pl.when(j < nact)
    def _():
        changed = jnp.logical_or(j == 0, tile_e_ref[j] != tile_e_ref[jnp.maximum(j - 1, 0)])

        @pl.when(changed)
        def _():
            wg_b[...] = wg_ref[...].astype(BF16)
            wu_b[...] = wu_ref[...].astype(BF16)
            wd_b[...] = wd_ref[...].astype(BF16)

        pltpu.make_async_copy(x_hbm.at[pl.ds(0, EXP_TM)], xbuf.at[slot], sem.at[slot]).wait()
        xb = xbuf[slot].astype(BF16)
        gate = jnp.dot(xb, wg_b[...], preferred_element_type=F32)
        up = jnp.dot(xb, wu_b[...], preferred_element_type=F32)
        h = (gate * jax.nn.sigmoid(gate) * up).astype(BF16)
        o_ref[...] = jnp.dot(h, wd_b[...], preferred_element_type=F32)

    @pl.when(j >= nact)
    def _():
        o_ref[...] = jnp.zeros_like(o_ref)


def _experts(tile_e, nact, src, x, w_gate, w_up, w_down, layer):
    grid_spec = pltpu.PrefetchScalarGridSpec(
        num_scalar_prefetch=3,
        grid=(N_EXP_TILES,),
        in_specs=[pl.BlockSpec(memory_space=pl.ANY),
                  pl.BlockSpec((None, None, D, D_EXPERT), lambda j, te, na, sr: (layer, te[j], 0, 0)),
                  pl.BlockSpec((None, None, D, D_EXPERT), lambda j, te, na, sr: (layer, te[j], 0, 0)),
                  pl.BlockSpec((None, None, D_EXPERT, D), lambda j, te, na, sr: (layer, te[j], 0, 0))],
        out_specs=pl.BlockSpec((EXP_TM, D), lambda j, te, na, sr: (j, 0)),
        scratch_shapes=[pltpu.VMEM((2, EXP_TM, D), F32),
                        pltpu.SemaphoreType.DMA((2,)),
                        pltpu.VMEM((D, D_EXPERT), BF16),
                        pltpu.VMEM((D, D_EXPERT), BF16),
                        pltpu.VMEM((D_EXPERT, D), BF16)])
    return pl.pallas_call(
        _experts_body,
        grid_spec=grid_spec,
        out_shape=jax.ShapeDtypeStruct((SORTED_ROWS, D), F32),
        compiler_params=_cparams(1),
        name="moe_experts",
    )(tile_e, nact, src, x, w_gate, w_up, w_down)


def _combine_body(pos1_ref, pos2_ref, x_ref, wts_ref, g_ref, b_ref, ys_hbm, o_ref, ybuf, sem):
    i = pl.program_id(0)
    slot = i % 2

    def gather_rows(step, s):
        def body(r, carry):
            t = step * COMB_TM + r
            pltpu.make_async_copy(ys_hbm.at[pl.ds(pos1_ref[t], 1)], ybuf.at[s, 0, pl.ds(r, 1)], sem.at[s]).start()
            pltpu.make_async_copy(ys_hbm.at[pl.ds(pos2_ref[t], 1)], ybuf.at[s, 1, pl.ds(r, 1)], sem.at[s]).start()
            return carry
        lax.fori_loop(0, COMB_TM, body, 0)

    @pl.when(i == 0)
    def _():
        gather_rows(0, 0)

    @pl.when(i + 1 < pl.num_programs(0))
    def _():
        gather_rows(i + 1, 1 - slot)

    for k in range(TOP_K):
        pltpu.make_async_copy(ys_hbm.at[pl.ds(0, COMB_TM)], ybuf.at[slot, k], sem.at[slot]).wait()
    y = wts_ref[:, 0:1] * ybuf[slot, 0] + wts_ref[:, 1:2] * ybuf[slot, 1]
    o_ref[...] = _ln(ALPHA * x_ref[...] + y, g_ref[...], b_ref[...])


def _combine(pos1, pos2, x, wts, g, b, ys):
    m = x.shape[0]
    grid_spec = pltpu.PrefetchScalarGridSpec(
        num_scalar_prefetch=2,
        grid=(m // COMB_TM,),
        in_specs=[pl.BlockSpec((COMB_TM, D), lambda i, p1, p2: (i, 0)),
                  pl.BlockSpec((COMB_TM, ROUTE_LANES), lambda i, p1, p2: (i, 0)),
                  pl.BlockSpec((1, D), lambda i, p1, p2: (0, 0)),
                  pl.BlockSpec((1, D), lambda i, p1, p2: (0, 0)),
                  pl.BlockSpec(memory_space=pl.ANY)],
        out_specs=pl.BlockSpec((COMB_TM, D), lambda i, p1, p2: (i, 0)),
        scratch_shapes=[pltpu.VMEM((2, TOP_K, COMB_TM, D), F32),
                        pltpu.SemaphoreType.DMA((2,))])
    return pl.pallas_call(
        _combine_body,
        grid_spec=grid_spec,
        out_shape=jax.ShapeDtypeStruct((m, D), F32),
        compiler_params=_cparams(1),
        name="moe_combine",
    )(pos1, pos2, x, wts, g, b, ys)


def _moe(x, wr, br, w_gate, w_up, w_down, layer, g, b):
    idx, wts, cnt = _route(x, wr, br)
    src, pos1, pos2, tile_e, nact = _plan(idx[0], idx[1], idx[2], idx[3], cnt.reshape(ROUTE_LANES))
    ys = _experts(tile_e, nact, src, x, w_gate, w_up, w_down, layer)
    return _combine(pos1, pos2, x, wts, g, b, ys)


def _router_params(w_rg, b_rg, w_re, b_re):
    w = jnp.concatenate([w_rg, jnp.transpose(w_re, (1, 0, 2)).reshape(D, N_EXPERTS)], axis=1)
    bias = jnp.concatenate([b_rg, b_re.reshape(N_EXPERTS)])
    pad = ROUTE_LANES - w.shape[1]
    return jnp.pad(w, ((0, 0), (0, pad))), jnp.pad(bias, (0, pad))[None, :]


def kernel(x_prompt, x_sample, mem_prompt, cache_mem_k, cache_mem_v, state_conv3, state_conv31,
           ln_gain, ln_bias, w_in_ab, w_conv3, ln_sgu_gain, ln_sgu_bias, w_spatial, b_spatial, w_out_ab,
           w_pw1, w_dw31, b_dw31, ln_conf_gain, ln_conf_bias, w_pw2,
           w_mem_q, w_mem_k, w_mem_v, w_mem_o,
           w_route_group, b_route_group, w_route_expert, b_route_expert,
           w_exp_gate, w_exp_up, w_exp_down):
    depth = ln_gain.shape[0]
    x = jnp.concatenate([x_prompt.reshape(N_PROMPT, D), x_sample.reshape(N_STREAMS * STEP, D)], axis=0)

    mem = mem_prompt.reshape(N_MEM, D)
    pk = _mem_kv(mem, w_mem_k, "mem_k")
    pv = _mem_kv(mem, w_mem_v, "mem_v")
    ck = cache_mem_k.reshape(depth, N_STREAMS, N_MEM, D)
    cv = cache_mem_v.reshape(depth, N_STREAMS, N_MEM, D)

    def ln_params(l, k):
        return ln_gain[l, k][None, :], ln_bias[l, k][None, :]

    c3_new = c31_new = v_rows = None
    for l in range(depth):
        i = l // 2
        if l % 2 == 0:
            proj = _mm(x, w_in_ab, i, F32, 1024, 512, "proj_in_ab")
            y_mix, c3_new, v_rows = _mix0(
                proj, state_conv3[i], w_conv3[i], ln_sgu_gain[i][None, :], ln_sgu_bias[i][None, :],
                w_spatial[i], b_spatial[i][:, :, None])
            w_out = w_out_ab
        else:
            proj = _mm(x, w_pw1, i, F32, 1024, 512, "proj_pw1")
            y_mix, c31_new = _mix1(
                proj, state_conv31[i], w_dw31[i], b_dw31[i][None, :],
                ln_conf_gain[i][None, :], ln_conf_bias[i][None, :])
            w_out = w_pw2
        x = _mm_ln(y_mix, w_out, i, x, *ln_params(l, 0), name="mix_out_ln")
        q = _mm(x, w_mem_q, l, BF16, 1024, 512, "mem_q")
        o = _attn(q, pk, pv, ck, cv, l)
        x = _mm_ln(o, w_mem_o, l, x, *ln_params(l, 1), name="mem_out_ln")
        wr, br = _router_params(w_route_group[l], b_route_group[l], w_route_expert[l], b_route_expert[l])
        x = _moe(x, wr, br, w_exp_gate, w_exp_up, w_exp_down, l, *ln_params(l, 2))

    y_prompt = x[:N_PROMPT].reshape(1, N_PROMPT, D)
    y_sample = x[N_PROMPT:].reshape(N_STREAMS, STEP, D)
    prompt_mem_k = pk.reshape(depth, 1, N_MEM, HEADS, HEAD_DIM)
    prompt_mem_v = pv.reshape(depth, 1, N_MEM, HEADS, HEAD_DIM)
    prompt_conv3 = c3_new[0].reshape(1, 1, 2, D_A)
    sample_conv3 = c3_new[1:].reshape(1, N_STREAMS, 2, D_A)
    prompt_conv31 = c31_new[0].reshape(1, 1, CONF_W - 1, D)
    sample_conv31 = c31_new[1:].reshape(1, N_STREAMS, CONF_W - 1, D)
    sample_sgu_v = v_rows.reshape(1, N_STREAMS, STEP, D_B)
    return (y_prompt, y_sample, prompt_mem_k, prompt_mem_v, prompt_conv3, prompt_conv31,
            sample_conv3, sample_conv31, sample_sgu_v)
```

```python
import functools

import jax
import jax.numpy as jnp
from jax import lax
from jax.experimental import pallas as pl
from jax.experimental.pallas import tpu as pltpu

F32 = jnp.float32
BF16 = jnp.bfloat16

D = 2048
N_PROMPT = 8192
N_STREAMS = 16
STEP = 64
NP_STEPS = N_PROMPT // STEP
M_TOK = N_PROMPT + N_STREAMS * STEP
N_MEM = 256
HEADS = 4
HEAD_DIM = D // HEADS
D_A = 1024
D_B = 1024
SGU_HEADS = 8
N_GROUPS = 4
E_PER = 4
N_EXPERTS = 16
D_EXPERT = 512
LN_EPS = 1e-5
ALPHA = (2.0 * 2) ** 0.25
ATTN_SCALE = HEAD_DIM ** -0.5
ROUTE_LANES = 128
NEG = -1e30

VMEM_LIMIT_BYTES = 60000 * 1024


def _cparams(n_axes):
    return pltpu.CompilerParams(
        dimension_semantics=("arbitrary",) * n_axes,
        vmem_limit_bytes=VMEM_LIMIT_BYTES)


def _ln(x, g, b):
    mu = jnp.mean(x, axis=-1, keepdims=True)
    xc = x - mu
    var = jnp.mean(xc * xc, axis=-1, keepdims=True)
    return xc * lax.rsqrt(var + LN_EPS) * g + b


def _bdot(a, b):
    return jnp.dot(a.astype(BF16), b.astype(BF16), preferred_element_type=F32)


def _conv_operand(x):
    return x.astype(BF16).astype(F32)


def _mm_body(x_ref, w_ref, o_ref):
    o_ref[...] = _bdot(x_ref[...], w_ref[...]).astype(o_ref.dtype)


def _mm(x, w, layer, out_dtype, tm, tn, name):
    m, k = x.shape
    n = w.shape[-1]
    return pl.pallas_call(
        _mm_body,
        grid=(m // tm, n // tn),
        in_specs=[pl.BlockSpec((tm, k), lambda i, j: (i, 0)),
                  pl.BlockSpec((None, k, tn), lambda i, j: (layer, 0, j))],
        out_specs=pl.BlockSpec((tm, tn), lambda i, j: (i, j)),
        out_shape=jax.ShapeDtypeStruct((m, n), out_dtype),
        compiler_params=_cparams(2),
        name=name,
    )(x, w)


def _mem_kv(mem, w, name):
    depth, k, n = w.shape
    tn = 512
    return pl.pallas_call(
        _mm_body,
        grid=(depth, n // tn),
        in_specs=[pl.BlockSpec((N_MEM, k), lambda l, j: (0, 0)),
                  pl.BlockSpec((None, k, tn), lambda l, j: (l, 0, j))],
        out_specs=pl.BlockSpec((None, N_MEM, tn), lambda l, j: (l, 0, j)),
        out_shape=jax.ShapeDtypeStruct((depth, N_MEM, n), F32),
        compiler_params=_cparams(2),
        name=name,
    )(mem, w)


MM_LN_TM = 512
W_STAGE_ROWS = 512


def _mm_ln_body(x_ref, r_ref, g_ref, b_ref, w_hbm, o_ref, wb, stage, acc0, acc1, sem, *, layer):
    i = pl.program_id(0)

    @pl.when(i == 0)
    def _():
        acc1[...] = jnp.zeros_like(acc1)
        for k in range(wb.shape[0] // W_STAGE_ROWS):
            rows = pl.ds(k * W_STAGE_ROWS, W_STAGE_ROWS)
            cp = pltpu.make_async_copy(w_hbm.at[layer, rows], stage, sem)
            cp.start()
            cp.wait()
            wb[k * W_STAGE_ROWS:(k + 1) * W_STAGE_ROWS, :] = stage[...].astype(BF16)

    def step(acc_new, acc_old):
        acc_new[...] = jnp.dot(x_ref[...], wb[...], preferred_element_type=F32)
        o_ref[...] = _ln(ALPHA * r_ref[...] + acc_old[...], g_ref[...], b_ref[...])

    @pl.when(i % 2 == 0)
    def _():
        step(acc0, acc1)

    @pl.when(i % 2 == 1)
    def _():
        step(acc1, acc0)


def _mm_ln(x, w, layer, res, g, b, name):
    m, k = x.shape
    n = w.shape[-1]
    tm = MM_LN_TM
    n_tiles = m // tm
    assert x.dtype == BF16

    def done_tile(i):
        return (jnp.maximum(i - 1, 0), 0)

    return pl.pallas_call(
        functools.partial(_mm_ln_body, layer=layer),
        grid=(n_tiles + 1,),
        in_specs=[pl.BlockSpec((tm, k), lambda i: (jnp.minimum(i, n_tiles - 1), 0)),
                  pl.BlockSpec((tm, n), done_tile),
                  pl.BlockSpec((1, n), lambda i: (0, 0)),
                  pl.BlockSpec((1, n), lambda i: (0, 0)),
                  pl.BlockSpec(memory_space=pl.ANY)],
        out_specs=pl.BlockSpec((tm, n), done_tile),
        out_shape=jax.ShapeDtypeStruct((m, n), F32),
        scratch_shapes=[pltpu.VMEM((k, n), BF16),
                        pltpu.VMEM((W_STAGE_ROWS, n), F32),
                        pltpu.VMEM((tm, n), F32),
                        pltpu.VMEM((tm, n), F32),
                        pltpu.SemaphoreType.DMA],
        compiler_params=_cparams(1),
        name=name,
    )(x, res, g, b, w)


def _state_block(i):
    return jnp.maximum(i - NP_STEPS, 0)


def _new_state_block(i):
    return jnp.where(i < NP_STEPS, 0, i - NP_STEPS + 1)


def _mix0_body(p_ref, st_ref, wc_ref, g_ref, b_ref, ws_ref, bs_ref,
               y_ref, c3_ref, vrow_ref, ctx_ref, vprev_ref):
    i = pl.program_id(0)

    z = p_ref[:, 2 * D_A:3 * D_A] * p_ref[:, 0:D_A]

    @pl.when(i == 0)
    def _():
        ctx_ref[0:8, :] = jnp.zeros((8, D_A), F32)
        vprev_ref[...] = jnp.zeros((STEP, D_B), BF16)

    @pl.when(i >= NP_STEPS)
    def _():
        ctx_ref[6:8, :] = _conv_operand(st_ref[...])

    zc = _conv_operand(z)
    ctx_ref[8:8 + STEP, :] = zc
    conv = (wc_ref[0:1, :] * ctx_ref[6:6 + STEP, :]
            + wc_ref[1:2, :] * ctx_ref[7:7 + STEP, :]
            + wc_ref[2:3, :] * zc)
    y_ref[:, 0:D_A] = (p_ref[:, D_A:2 * D_A] * conv).astype(y_ref.dtype)
    c3_ref[...] = z[STEP - 2:STEP, :]
    ctx_ref[0:8, :] = ctx_ref[STEP:STEP + 8, :]

    u = jax.nn.gelu(p_ref[:, 3 * D_A:3 * D_A + D_B])
    v = _ln(jax.nn.gelu(p_ref[:, 3 * D_A + D_B:3 * D_A + 2 * D_B]), g_ref[...], b_ref[...])
    vrow_ref[...] = v
    vb = v.astype(BF16)
    second_chunk = jnp.logical_and(i < NP_STEPS, i % 2 == 1)
    row = lax.broadcasted_iota(jnp.int32, (2 * STEP, 2 * STEP), 0)
    col = lax.broadcasted_iota(jnp.int32, (2 * STEP, 2 * STEP), 1)
    mask = (col // STEP) <= (row // STEP)
    for h in range(SGU_HEADS):
        hs = slice(h * 128, (h + 1) * 128)
        wm = jnp.where(mask, ws_ref[h], 0.0).astype(BF16)
        v_h = vb[:, hs]
        f_first = jnp.dot(wm[0:STEP, 0:STEP], v_h, preferred_element_type=F32) + bs_ref[h, 0:STEP, :]
        f_second = (jnp.dot(wm[STEP:, 0:STEP], vprev_ref[:, hs], preferred_element_type=F32)
                    + jnp.dot(wm[STEP:, STEP:], v_h, preferred_element_type=F32)
                    + bs_ref[h, STEP:, :])
        f = jnp.where(second_chunk, f_second, f_first)
        y_ref[:, D_A + h * 128:D_A + (h + 1) * 128] = (u[:, hs] * f).astype(y_ref.dtype)
    vprev_ref[...] = vb


def _mix0(proj, state_c3, w_conv3, g, b, w_spatial, b_spatial_col):
    n_steps = NP_STEPS + N_STREAMS
    return pl.pallas_call(
        _mix0_body,
        grid=(n_steps,),
        in_specs=[pl.BlockSpec((STEP, 3 * D_A + 2 * D_B), lambda i: (i, 0)),
                  pl.BlockSpec((None, 2, D_A), lambda i: (_state_block(i), 0, 0)),
                  pl.BlockSpec((3, D_A), lambda i: (0, 0)),
                  pl.BlockSpec((1, D_B), lambda i: (0, 0)),
                  pl.BlockSpec((1, D_B), lambda i: (0, 0)),
                  pl.BlockSpec((SGU_HEADS, 128, 128), lambda i: (0, 0, 0)),
                  pl.BlockSpec((SGU_HEADS, 128, 1), lambda i: (0, 0, 0))],
        out_specs=[pl.BlockSpec((STEP, D), lambda i: (i, 0)),
                   pl.BlockSpec((None, 2, D_A), lambda i: (_new_state_block(i), 0, 0)),
                   pl.BlockSpec((None, STEP, D_B), lambda i: (_state_block(i), 0, 0))],
        out_shape=[jax.ShapeDtypeStruct((M_TOK, D), BF16),
                   jax.ShapeDtypeStruct((N_STREAMS + 1, 2, D_A), F32),
                   jax.ShapeDtypeStruct((N_STREAMS, STEP, D_B), F32)],
        scratch_shapes=[pltpu.VMEM((STEP + 8, D_A), F32),
                        pltpu.VMEM((STEP, D_B), BF16)],
        compiler_params=_cparams(1),
        name="mix_even",
    )(proj, state_c3, w_conv3, g, b, w_spatial, b_spatial_col)


CONF_W = 31
CTX_PAD = 32
CONV_COLS = 512
SUBLANES = 8
SHIFT_ROWS = CTX_PAD + STEP - SUBLANES


def _mix1_body(p_ref, st_ref, w_ref, bdw_ref, g_ref, b_ref, y_ref, c31_ref, ctx_ref, cv_ref, sh_ref):
    i = pl.program_id(0)
    glu = p_ref[:, 0:D] * jax.nn.sigmoid(p_ref[:, D:2 * D])

    @pl.when(i == 0)
    def _():
        ctx_ref[0:CTX_PAD, :] = jnp.zeros((CTX_PAD, D), F32)

    @pl.when(i >= NP_STEPS)
    def _():
        ctx_ref[CTX_PAD - (CONF_W - 1):CTX_PAD, :] = _conv_operand(st_ref[...])

    ctx_ref[CTX_PAD:CTX_PAD + STEP, :] = _conv_operand(glu)
    c31_ref[...] = glu[STEP - (CONF_W - 1):STEP, :]
    first = CTX_PAD - (CONF_W - 1)
    for c in range(D // CONV_COLS):
        cs = slice(c * CONV_COLS, (c + 1) * CONV_COLS)
        for s in range(1, SUBLANES):
            sh_ref[s - 1, :, :] = ctx_ref[s:s + SHIFT_ROWS, cs]
        acc = None
        for k in range(CONF_W):
            q, s = divmod(first + k, SUBLANES)
            rows = slice(q * SUBLANES, q * SUBLANES + STEP)
            operand = ctx_ref[rows, cs] if s == 0 else sh_ref[s - 1, rows, :]
            term = w_ref[k:k + 1, cs] * operand
            acc = term if acc is None else acc + term
        cv_ref[:, cs] = acc + bdw_ref[:, cs]
    c = _ln(cv_ref[...], g_ref[...], b_ref[...])
    y_ref[...] = (c * jax.nn.sigmoid(c)).astype(y_ref.dtype)
    ctx_ref[0:CTX_PAD, :] = ctx_ref[STEP:STEP + CTX_PAD, :]


def _mix1(proj, state_c31, w_dw, b_dw, g, b):
    n_steps = NP_STEPS + N_STREAMS
    return pl.pallas_call(
        _mix1_body,
        grid=(n_steps,),
        in_specs=[pl.BlockSpec((STEP, 2 * D), lambda i: (i, 0)),
                  pl.BlockSpec((None, CONF_W - 1, D), lambda i: (_state_block(i), 0, 0)),
                  pl.BlockSpec((CONF_W, D), lambda i: (0, 0)),
                  pl.BlockSpec((1, D), lambda i: (0, 0)),
                  pl.BlockSpec((1, D), lambda i: (0, 0)),
                  pl.BlockSpec((1, D), lambda i: (0, 0))],
        out_specs=[pl.BlockSpec((STEP, D), lambda i: (i, 0)),
                   pl.BlockSpec((None, CONF_W - 1, D), lambda i: (_new_state_block(i), 0, 0))],
        out_shape=[jax.ShapeDtypeStruct((M_TOK, D), BF16),
                   jax.ShapeDtypeStruct((N_STREAMS + 1, CONF_W - 1, D), F32)],
        scratch_shapes=[pltpu.VMEM((CTX_PAD + STEP, D), F32),
                        pltpu.VMEM((STEP, D), F32),
                        pltpu.VMEM((SUBLANES - 1, SHIFT_ROWS, CONV_COLS), F32)],
        compiler_params=_cparams(1),
        name="mix_odd",
    )(proj, state_c31, w_dw, b_dw, g, b)


def _attn_body(q_ref, k_ref, v_ref, *rest):
    o_ref = rest[-1]
    for h in range(HEADS):
        hs = slice(h * HEAD_DIM, (h + 1) * HEAD_DIM)
        s = lax.dot_general(q_ref[:, hs], k_ref[:, hs].astype(BF16),
                            (((1,), (1,)), ((), ())), preferred_element_type=F32) * ATTN_SCALE
        p = jnp.exp(s - jnp.max(s, axis=-1, keepdims=True))
        p = p / jnp.sum(p, axis=-1, keepdims=True)
        o_ref[:, hs] = _bdot(p, v_ref[:, hs]).astype(o_ref.dtype)


def _attn_prompt_body(q_ref, k_ref, v_ref, o_ref, *, n_prompt_steps):
    @pl.when(pl.program_id(0) < n_prompt_steps)
    def _():
        _attn_body(q_ref, k_ref, v_ref, o_ref)

    @pl.when(pl.program_id(0) >= n_prompt_steps)
    def _():
        o_ref[...] = jnp.zeros_like(o_ref)


def _attn(q, pk, pv, ck, cv, layer):
    tq = 512
    o = pl.pallas_call(
        functools.partial(_attn_prompt_body, n_prompt_steps=N_PROMPT // tq),
        grid=(M_TOK // tq,),
        in_specs=[pl.BlockSpec((tq, D), lambda i: (i, 0)),
                  pl.BlockSpec((None, N_MEM, D), lambda i: (layer, 0, 0)),
                  pl.BlockSpec((None, N_MEM, D), lambda i: (layer, 0, 0))],
        out_specs=pl.BlockSpec((tq, D), lambda i: (i, 0)),
        out_shape=jax.ShapeDtypeStruct((M_TOK, D), BF16),
        compiler_params=_cparams(1),
        name="attn_prompt",
    )(q, pk, pv)
    return pl.pallas_call(
        _attn_body,
        grid=(N_STREAMS,),
        in_specs=[pl.BlockSpec((STEP, D), lambda n: (NP_STEPS + n, 0)),
                  pl.BlockSpec((None, None, N_MEM, D), lambda n: (layer, n, 0, 0)),
                  pl.BlockSpec((None, None, N_MEM, D), lambda n: (layer, n, 0, 0)),
                  pl.BlockSpec(memory_space=pl.ANY)],
        out_specs=pl.BlockSpec((STEP, D), lambda n: (NP_STEPS + n, 0)),
        out_shape=jax.ShapeDtypeStruct((M_TOK, D), BF16),
        input_output_aliases={3: 0},
        compiler_params=_cparams(1),
        name="attn_sample",
    )(q, ck, cv, o)


TOP_K = 2
ROUTE_TM = 512
EXP_TM = 256
N_ASSIGN = M_TOK * TOP_K
N_EXP_TILES = N_ASSIGN // EXP_TM
N_VISITS = N_EXP_TILES + N_EXPERTS - 1
MOVE_TM = 256
LANE = 128
CHUNKS = D // LANE
ISSUE_UNROLL = 8


def _to_chunk_major(dst_ref, x, rows):
    for c in range(CHUNKS):
        dst_ref[pl.ds(c, rows, stride=CHUNKS), :] = x[:, c * LANE:(c + 1) * LANE]


def _from_chunk_major(src_ref, rows):
    return jnp.concatenate([src_ref[pl.ds(c, rows, stride=CHUNKS), :] for c in range(CHUNKS)], axis=-1)


def _route_body(x_ref, wr_ref, br_ref, idx_ref, wts_ref, cnt_ref, run_ref):
    i = pl.program_id(0)

    @pl.when(i == 0)
    def _():
        run_ref[...] = jnp.zeros_like(run_ref)

    logits = _bdot(x_ref[...], wr_ref[...]) + br_ref[...]
    lane = lax.broadcasted_iota(jnp.int32, logits.shape, 1)

    def first_argmax(vals):
        top = jnp.max(vals, axis=-1, keepdims=True)
        idx = jnp.min(jnp.where(vals == top, lane, ROUTE_LANES), axis=-1, keepdims=True)
        return top, idx

    g_logits = jnp.where(lane < N_GROUPS, logits, NEG)
    g_top, g_idx = first_argmax(g_logits)
    p_group = 1.0 / jnp.sum(jnp.exp(g_logits - g_top), axis=-1, keepdims=True)
    lo_lane = N_GROUPS + E_PER * g_idx
    e_logits = jnp.where(jnp.logical_and(lane >= lo_lane, lane < lo_lane + E_PER), logits, NEG)
    v1, i1 = first_argmax(e_logits)
    v2, i2 = first_argmax(jnp.where(lane == i1, NEG, e_logits))
    t = jnp.exp(v2 - v1)
    w1 = p_group / (1.0 + t)
    w2 = p_group * t / (1.0 + t)

    a1 = (lane == i1).astype(F32)
    a2 = (lane == i2).astype(F32)
    a = a1 + a2
    tm = logits.shape[0]
    earlier = (lax.broadcasted_iota(jnp.int32, (tm, tm), 1)
               < lax.broadcasted_iota(jnp.int32, (tm, tm), 0)).astype(BF16)
    before = jnp.dot(earlier, a.astype(BF16), preferred_element_type=F32) + run_ref[...]
    r1 = jnp.sum(a1 * before, axis=-1, keepdims=True)
    r2 = jnp.sum(a2 * before, axis=-1, keepdims=True)
    run_ref[...] += jnp.sum(a, axis=0, keepdims=True)
    cnt_ref[...] = run_ref[...].astype(jnp.int32)

    e1 = (i1 - N_GROUPS).astype(F32)
    e2 = (i2 - N_GROUPS).astype(F32)
    packed = (jnp.where(lane == 0, e1, 0.0) + jnp.where(lane == 1, e2, 0.0)
              + jnp.where(lane == 2, r1, 0.0) + jnp.where(lane == 3, r2, 0.0))
    idx_ref[...] = packed.T[0:8, :].astype(jnp.int32)
    wts_ref[...] = jnp.where(lane == 0, w1, 0.0) + jnp.where(lane == 1, w2, 0.0)


def _route(x, wr, br):
    m = x.shape[0]
    tm = ROUTE_TM
    return pl.pallas_call(
        _route_body,
        grid=(m // tm,),
        in_specs=[pl.BlockSpec((tm, D), lambda i: (i, 0)),
                  pl.BlockSpec((D, ROUTE_LANES), lambda i: (0, 0)),
                  pl.BlockSpec((1, ROUTE_LANES), lambda i: (0, 0))],
        out_specs=[pl.BlockSpec((8, tm), lambda i: (0, i)),
                   pl.BlockSpec((tm, ROUTE_LANES), lambda i: (i, 0)),
                   pl.BlockSpec((1, ROUTE_LANES), lambda i: (0, 0))],
        out_shape=[jax.ShapeDtypeStruct((8, m), jnp.int32),
                   jax.ShapeDtypeStruct((m, ROUTE_LANES), F32),
                   jax.ShapeDtypeStruct((1, ROUTE_LANES), jnp.int32)],
        scratch_shapes=[pltpu.VMEM((1, ROUTE_LANES), F32)],
        compiler_params=_cparams(1),
        name="moe_route",
    )(x, wr, br)


def _plan_body(cnt_ref, idx_ref, pos_ref, vt_ref, ve_ref, vlo_ref, vhi_ref, nvis_ref):
    shift = EXP_TM.bit_length() - 1
    e1 = idx_ref[0:1, :]
    e2 = idx_ref[1:2, :]
    pos1 = idx_ref[2:3, :]
    pos2 = idx_ref[3:4, :]
    start = jnp.int32(0)
    n_vis = jnp.int32(0)
    last_e = jnp.int32(0)
    for e in range(N_EXPERTS):
        cnt = cnt_ref[N_GROUPS + e]
        end = start + cnt
        pos1 = pos1 + jnp.where(e1 == e, start, 0)
        pos2 = pos2 + jnp.where(e2 == e, start, 0)
        first_tile = lax.shift_right_logical(start, shift)
        n_tiles = jnp.where(cnt > 0, lax.shift_right_logical(end + (EXP_TM - 1), shift) - first_tile, 0)

        def add_visit(k, carry, e=e, start=start, end=end, first_tile=first_tile, n_vis=n_vis):
            tile = first_tile + k
            vt_ref[n_vis + k] = tile
            ve_ref[n_vis + k] = e
            vlo_ref[n_vis + k] = jnp.maximum(start - tile * EXP_TM, 0)
            vhi_ref[n_vis + k] = jnp.minimum(end - tile * EXP_TM, EXP_TM)
            return carry

        lax.fori_loop(0, n_tiles, add_visit, 0)
        n_vis = n_vis + n_tiles
        last_e = jnp.where(cnt > 0, e, last_e)
        start = end
    nvis_ref[0] = n_vis

    def add_idle(k, carry):
        vt_ref[k] = N_EXP_TILES - 1
        ve_ref[k] = last_e
        vlo_ref[k] = 0
        vhi_ref[k] = 0
        return carry

    lax.fori_loop(n_vis, N_VISITS, add_idle, 0)
    pos_ref[...] = jnp.concatenate([pos1, pos2, jnp.zeros((6, M_TOK), jnp.int32)], axis=0)


def _plan(cnt, idx):
    smem = pl.BlockSpec(memory_space=pltpu.SMEM)
    vmem = pl.BlockSpec(memory_space=pltpu.VMEM)
    table = jax.ShapeDtypeStruct((N_VISITS,), jnp.int32)
    return pl.pallas_call(
        _plan_body,
        in_specs=[smem, vmem],
        out_specs=[vmem, smem, smem, smem, smem, smem],
        out_shape=[jax.ShapeDtypeStruct((8, M_TOK), jnp.int32), table, table, table, table,
                   jax.ShapeDtypeStruct((1,), jnp.int32)],
        name="moe_plan",
    )(cnt, idx)


def _row_copy(src_ref, src_row, dst_ref, dst_row, sem):
    return pltpu.make_async_copy(src_ref.at[pl.ds(src_row * CHUNKS, CHUNKS)],
                                 dst_ref.at[pl.ds(dst_row * CHUNKS, CHUNKS)], sem)


def _scatter_body(pos1_ref, pos2_ref, x_ref, xs_hbm, stage, sem):
    i = pl.program_id(0)
    last = pl.num_programs(0) - 1
    slot = i % 2

    def wait_slot(s):
        for _ in range(TOP_K):
            pltpu.make_async_copy(stage.at[s], xs_hbm.at[pl.ds(0, MOVE_TM * CHUNKS)], sem.at[s]).wait()

    @pl.when(i >= 2)
    def _():
        wait_slot(slot)

    _to_chunk_major(stage.at[slot], x_ref[...], MOVE_TM)

    def issue(r, carry):
        t = i * MOVE_TM + r
        _row_copy(stage.at[slot], r, xs_hbm, pos1_ref[t], sem.at[slot]).start()
        _row_copy(stage.at[slot], r, xs_hbm, pos2_ref[t], sem.at[slot]).start()
        return carry

    lax.fori_loop(0, MOVE_TM, issue, 0, unroll=ISSUE_UNROLL)

    @pl.when(i == last)
    def _():
        wait_slot(1 - slot)
        wait_slot(slot)


def _scatter(pos1, pos2, x):
    m = x.shape[0]
    assert (m // MOVE_TM) >= 2
    grid_spec = pltpu.PrefetchScalarGridSpec(
        num_scalar_prefetch=2,
        grid=(m // MOVE_TM,),
        in_specs=[pl.BlockSpec((MOVE_TM, D), lambda i, p1, p2: (i, 0))],
        out_specs=pl.BlockSpec(memory_space=pl.ANY),
        scratch_shapes=[pltpu.VMEM((2, MOVE_TM * CHUNKS, LANE), F32),
                        pltpu.SemaphoreType.DMA((2,))])
    return pl.pallas_call(
        _scatter_body,
        grid_spec=grid_spec,
        out_shape=jax.ShapeDtypeStruct((N_ASSIGN * CHUNKS, LANE), F32),
        compiler_params=_cparams(1),
        name="moe_scatter",
    )(pos1, pos2, x)


def _experts_body(vt_ref, ve_ref, vlo_ref, vhi_ref, nvis_ref, x_ref, wg_ref, wu_ref, wd_ref,
                  o_ref, wg_b, wu_b, wd_b):
    v = pl.program_id(0)
    prev = jnp.maximum(v - 1, 0)

    @pl.when(v < nvis_ref[0])
    def _():
        @pl.when(jnp.logical_or(v == 0, ve_ref[v] != ve_ref[prev]))
        def _():
            wg_b[...] = wg_ref[...].astype(BF16)
            wu_b[...] = wu_ref[...].astype(BF16)
            wd_b[...] = wd_ref[...].astype(BF16)

        xb = _from_chunk_major(x_ref, EXP_TM).astype(BF16)
        gate = jnp.dot(xb, wg_b[...], preferred_element_type=F32)
        up = jnp.dot(xb, wu_b[...], preferred_element_type=F32)
        h = (gate * jax.nn.sigmoid(gate) * up).astype(BF16)
        y = jnp.dot(h, wd_b[...], preferred_element_type=F32)
        row = lax.broadcasted_iota(jnp.int32, (EXP_TM, 1), 0)
        mine = jnp.logical_and(row >= vlo_ref[v], row < vhi_ref[v])
        first_visit = jnp.logical_or(v == 0, vt_ref[v] != vt_ref[prev])

        @pl.when(first_visit)
        def _():
            _to_chunk_major(o_ref, jnp.where(mine, y, 0.0), EXP_TM)

        @pl.when(jnp.logical_not(first_visit))
        def _():
            _to_chunk_major(o_ref, jnp.where(mine, y, _from_chunk_major(o_ref, EXP_TM)), EXP_TM)


def _experts(vt, ve, vlo, vhi, nvis, xs, w_gate, w_up, w_down, layer):
    def tile_map(v, vt, ve, vlo, vhi, nvis):
        return (vt[v], 0)

    def weight_map(v, vt, ve, vlo, vhi, nvis):
        return (layer, ve[v], 0, 0)

    grid_spec = pltpu.PrefetchScalarGridSpec(
        num_scalar_prefetch=5,
        grid=(N_VISITS,),
        in_specs=[pl.BlockSpec((EXP_TM * CHUNKS, LANE), tile_map),
                  pl.BlockSpec((None, None, D, D_EXPERT), weight_map),
                  pl.BlockSpec((None, None, D, D_EXPERT), weight_map),
                  pl.BlockSpec((None, None, D_EXPERT, D), weight_map)],
        out_specs=pl.BlockSpec((EXP_TM * CHUNKS, LANE), tile_map),
        scratch_shapes=[pltpu.VMEM((D, D_EXPERT), BF16),
                        pltpu.VMEM((D, D_EXPERT), BF16),
                        pltpu.VMEM((D_EXPERT, D), BF16)])
    return pl.pallas_call(
        _experts_body,
        grid_spec=grid_spec,
        out_shape=jax.ShapeDtypeStruct((N_ASSIGN * CHUNKS, LANE), F32),
        compiler_params=_cparams(1),
        name="moe_experts",
    )(vt, ve, vlo, vhi, nvis, xs, w_gate, w_up, w_down)


def _combine_body(pos1_ref, pos2_ref, x_ref, wts_ref, g_ref, b_ref, ys_hbm, o_ref, ybuf, sem):
    i = pl.program_id(0)
    slot = i % 2

    def gather_rows(step, s):
        def issue(r, carry):
            t = step * MOVE_TM + r
            _row_copy(ys_hbm, pos1_ref[t], ybuf.at[s, 0], r, sem.at[s]).start()
            _row_copy(ys_hbm, pos2_ref[t], ybuf.at[s, 1], r, sem.at[s]).start()
            return carry
        lax.fori_loop(0, MOVE_TM, issue, 0, unroll=ISSUE_UNROLL)

    @pl.when(i == 0)
    def _():
        gather_rows(0, 0)

    @pl.when(i + 1 < pl.num_programs(0))
    def _():
        gather_rows(i + 1, 1 - slot)

    for k in range(TOP_K):
        pltpu.make_async_copy(ys_hbm.at[pl.ds(0, MOVE_TM * CHUNKS)], ybuf.at[slot, k], sem.at[slot]).wait()
    y = (wts_ref[:, 0:1] * _from_chunk_major(ybuf.at[slot, 0], MOVE_TM)
         + wts_ref[:, 1:2] * _from_chunk_major(ybuf.at[slot, 1], MOVE_TM))
    o_ref[...] = _ln(ALPHA * x_ref[...] + y, g_ref[...], b_ref[...])


def _combine(pos1, pos2, x, wts, g, b, ys):
    m = x.shape[0]
    grid_spec = pltpu.PrefetchScalarGridSpec(
        num_scalar_prefetch=2,
        grid=(m // MOVE_TM,),
        in_specs=[pl.BlockSpec((MOVE_TM, D), lambda i, p1, p2: (i, 0)),
                  pl.BlockSpec((MOVE_TM, ROUTE_LANES), lambda i, p1, p2: (i, 0)),
                  pl.BlockSpec((1, D), lambda i, p1, p2: (0, 0)),
                  pl.BlockSpec((1, D), lambda i, p1, p2: (0, 0)),
                  pl.BlockSpec(memory_space=pl.ANY)],
        out_specs=pl.BlockSpec((MOVE_TM, D), lambda i, p1, p2: (i, 0)),
        scratch_shapes=[pltpu.VMEM((2, TOP_K, MOVE_TM * CHUNKS, LANE), F32),
                        pltpu.SemaphoreType.DMA((2,))])
    return pl.pallas_call(
        _combine_body,
        grid_spec=grid_spec,
        out_shape=jax.ShapeDtypeStruct((m, D), F32),
        compiler_params=_cparams(1),
        name="moe_combine",
    )(pos1, pos2, x, wts, g, b, ys)


def _moe(x, wr, br, w_gate, w_up, w_down, layer, g, b):
    idx, wts, cnt = _route(x, wr, br)
    pos, vt, ve, vlo, vhi, nvis = _plan(cnt.reshape(ROUTE_LANES), idx)
    xs = _scatter(pos[0], pos[1], x)
    ys = _experts(vt, ve, vlo, vhi, nvis, xs, w_gate, w_up, w_down, layer)
    return _combine(pos[0], pos[1], x, wts, g, b, ys)


def _router_params(w_rg, b_rg, w_re, b_re):
    w = jnp.concatenate([w_rg, jnp.transpose(w_re, (1, 0, 2)).reshape(D, N_EXPERTS)], axis=1)
    bias = jnp.concatenate([b_rg, b_re.reshape(N_EXPERTS)])
    pad = ROUTE_LANES - w.shape[1]
    return jnp.pad(w, ((0, 0), (0, pad))), jnp.pad(bias, (0, pad))[None, :]


def kernel(x_prompt, x_sample, mem_prompt, cache_mem_k, cache_mem_v, state_conv3, state_conv31,
           ln_gain, ln_bias, w_in_ab, w_conv3, ln_sgu_gain, ln_sgu_bias, w_spatial, b_spatial, w_out_ab,
           w_pw1, w_dw31, b_dw31, ln_conf_gain, ln_conf_bias, w_pw2,
           w_mem_q, w_mem_k, w_mem_v, w_mem_o,
           w_route_group, b_route_group, w_route_expert, b_route_expert,
           w_exp_gate, w_exp_up, w_exp_down):
    depth = ln_gain.shape[0]
    x = jnp.concatenate([x_prompt.reshape(N_PROMPT, D), x_sample.reshape(N_STREAMS * STEP, D)], axis=0)

    mem = mem_prompt.reshape(N_MEM, D)
    pk = _mem_kv(mem, w_mem_k, "mem_k")
    pv = _mem_kv(mem, w_mem_v, "mem_v")
    ck = cache_mem_k.reshape(depth, N_STREAMS, N_MEM, D)
    cv = cache_mem_v.reshape(depth, N_STREAMS, N_MEM, D)

    def ln_params(l, k):
        return ln_gain[l, k][None, :], ln_bias[l, k][None, :]

    c3_new = c31_new = v_rows = None
    for l in range(depth):
        i = l // 2
        if l % 2 == 0:
            proj = _mm(x, w_in_ab, i, F32, 1024, 512, "proj_in_ab")
            y_mix, c3_new, v_rows = _mix0(
                proj, state_conv3[i], w_conv3[i], ln_sgu_gain[i][None, :], ln_sgu_bias[i][None, :],
                w_spatial[i], b_spatial[i][:, :, None])
            w_out = w_out_ab
        else:
            proj = _mm(x, w_pw1, i, F32, 1024, 512, "proj_pw1")
            y_mix, c31_new = _mix1(
                proj, state_conv31[i], w_dw31[i], b_dw31[i][None, :],
                ln_conf_gain[i][None, :], ln_conf_bias[i][None, :])
            w_out = w_pw2
        x = _mm_ln(y_mix, w_out, i, x, *ln_params(l, 0), name="mix_out_ln")
        q = _mm(x, w_mem_q, l, BF16, 1024, 512, "mem_q")
        o = _attn(q, pk, pv, ck, cv, l)
        x = _mm_ln(o, w_mem_o, l, x, *ln_params(l, 1), name="mem_out_ln")
        wr, br = _router_params(w_route_group[l], b_route_group[l], w_route_expert[l], b_route_expert[l])
        x = _moe(x, wr, br, w_exp_gate, w_exp_up, w_exp_down, l, *ln_params(l, 2))

    y_prompt = x[:N_PROMPT].reshape(1, N_PROMPT, D)
    y_sample = x[N_PROMPT:].reshape(N_STREAMS, STEP, D)
    prompt_mem_k = pk.reshape(depth, 1, N_MEM, HEADS, HEAD_DIM)
    prompt_mem_v = pv.reshape(depth, 1, N_MEM, HEADS, HEAD_DIM)
    prompt_conv3 = c3_new[0].reshape(1, 1, 2, D_A)
    sample_conv3 = c3_new[1:].reshape(1, N_STREAMS, 2, D_A)
    prompt_conv31 = c31_new[0].reshape(1, 1, CONF_W - 1, D)
    sample_conv31 = c31_new[1:].reshape(1, N_STREAMS, CONF_W - 1, D)
    sample_sgu_v = v_rows.reshape(1, N_STREAMS, STEP, D_B)
    return (y_prompt, y_sample, prompt_mem_k, prompt_mem_v, prompt_conv3, prompt_conv31,
            sample_conv3, sample_conv31, sample_sgu_v)
```

```python
import functools

import jax
import jax.numpy as jnp
from jax import lax
from jax.experimental import pallas as pl
from jax.experimental.pallas import tpu as pltpu

F32 = jnp.float32
BF16 = jnp.bfloat16

D = 2048
N_PROMPT = 8192
N_STREAMS = 16
STEP = 64
NP_STEPS = N_PROMPT // STEP
M_TOK = N_PROMPT + N_STREAMS * STEP
N_MEM = 256
HEADS = 4
HEAD_DIM = D // HEADS
D_A = 1024
D_B = 1024
SGU_HEADS = 8
N_GROUPS = 4
E_PER = 4
N_EXPERTS = 16
D_EXPERT = 512
LN_EPS = 1e-5
ALPHA = (2.0 * 2) ** 0.25
ATTN_SCALE = HEAD_DIM ** -0.5
ROUTE_LANES = 128
NEG = -1e30

VMEM_LIMIT_BYTES = 60000 * 1024


def _cparams(n_axes):
    return pltpu.CompilerParams(
        dimension_semantics=("arbitrary",) * n_axes,
        vmem_limit_bytes=VMEM_LIMIT_BYTES)


def _ln(x, g, b):
    mu = jnp.mean(x, axis=-1, keepdims=True)
    xc = x - mu
    var = jnp.mean(xc * xc, axis=-1, keepdims=True)
    return xc * lax.rsqrt(var + LN_EPS) * g + b


def _bdot(a, b):
    return jnp.dot(a.astype(BF16), b.astype(BF16), preferred_element_type=F32)


def _conv_operand(x):
    return x.astype(BF16).astype(F32)


def _mm_body(x_ref, w_ref, o_ref):
    o_ref[...] = _bdot(x_ref[...], w_ref[...]).astype(o_ref.dtype)


def _mm(x, w, layer, out_dtype, tm, tn, name):
    m, k = x.shape
    n = w.shape[-1]
    return pl.pallas_call(
        _mm_body,
        grid=(m // tm, n // tn),
        in_specs=[pl.BlockSpec((tm, k), lambda i, j: (i, 0)),
                  pl.BlockSpec((None, k, tn), lambda i, j: (layer, 0, j))],
        out_specs=pl.BlockSpec((tm, tn), lambda i, j: (i, j)),
        out_shape=jax.ShapeDtypeStruct((m, n), out_dtype),
        compiler_params=_cparams(2),
        name=name,
    )(x, w)


def _mm_two_body(xa_ref, xb_ref, w_ref, o_ref, *, a_tiles):
    @pl.when(pl.program_id(0) < a_tiles)
    def _():
        _mm_body(xa_ref, w_ref, o_ref)

    @pl.when(pl.program_id(0) >= a_tiles)
    def _():
        _mm_body(xb_ref, w_ref, o_ref)


def _mm_two(xa, xb, w, layer, out_dtype, tm, tn, name):
    k = xa.shape[1]
    n = w.shape[-1]
    a_tiles, b_tiles = xa.shape[0] // tm, xb.shape[0] // tm
    return pl.pallas_call(
        functools.partial(_mm_two_body, a_tiles=a_tiles),
        grid=(a_tiles + b_tiles, n // tn),
        in_specs=[pl.BlockSpec((tm, k), lambda i, j: (jnp.minimum(i, a_tiles - 1), 0)),
                  pl.BlockSpec((tm, k), lambda i, j: (jnp.maximum(i - a_tiles, 0), 0)),
                  pl.BlockSpec((None, k, tn), lambda i, j: (layer, 0, j))],
        out_specs=pl.BlockSpec((tm, tn), lambda i, j: (i, j)),
        out_shape=jax.ShapeDtypeStruct((xa.shape[0] + xb.shape[0], n), out_dtype),
        compiler_params=_cparams(2),
        name=name,
    )(xa, xb, w)


def _mem_kv_body(x_ref, w_ref, o_ref, ob_ref):
    y = _bdot(x_ref[...], w_ref[...])
    o_ref[...] = y
    ob_ref[...] = y.astype(BF16)


def _mem_kv(mem, w, name):
    depth, k, n = w.shape
    tn = 512
    out_block = pl.BlockSpec((None, N_MEM, tn), lambda l, j: (l, 0, j))
    return pl.pallas_call(
        _mem_kv_body,
        grid=(depth, n // tn),
        in_specs=[pl.BlockSpec((N_MEM, k), lambda l, j: (0, 0)),
                  pl.BlockSpec((None, k, tn), lambda l, j: (l, 0, j))],
        out_specs=[out_block, out_block],
        out_shape=[jax.ShapeDtypeStruct((depth, N_MEM, n), F32),
                   jax.ShapeDtypeStruct((depth, N_MEM, n), BF16)],
        compiler_params=_cparams(2),
        name=name,
    )(mem, w)


MM_LN_TM = 512
W_STAGE_ROWS = 128


def _load_weight_bf16(w_hbm, wb, stage, sem):
    rows = stage.shape[1]
    n_chunks = wb.shape[0] // rows

    def copy(k):
        return pltpu.make_async_copy(w_hbm.at[pl.ds(k * rows, rows)], stage.at[k % 2], sem.at[k % 2])

    copy(0).start()
    for k in range(n_chunks):
        if k + 1 < n_chunks:
            copy(k + 1).start()
        copy(k).wait()
        wb[k * rows:(k + 1) * rows, :] = stage[k % 2].astype(BF16)


def _mm_ln_body(x_ref, *rest, layer, res_split):
    if res_split is None:
        r_ref, g_ref, b_ref, w_hbm, o_ref, wb, stage, acc0, acc1, sem = rest
    else:
        ra_ref, rb_ref, g_ref, b_ref, w_hbm, o_ref, wb, stage, acc0, acc1, sem = rest
    i = pl.program_id(0)

    @pl.when(i == 0)
    def _():
        acc1[...] = jnp.zeros_like(acc1)
        _load_weight_bf16(w_hbm.at[layer], wb, stage, sem)

    def step(acc_new, acc_old):
        acc_new[...] = jnp.dot(x_ref[...], wb[...], preferred_element_type=F32)
        if res_split is None:
            res = r_ref[...]
        else:
            res = jnp.where(i - 1 < res_split, ra_ref[...], rb_ref[...])
        o_ref[...] = _ln(ALPHA * res + acc_old[...], g_ref[...], b_ref[...])

    @pl.when(i % 2 == 0)
    def _():
        step(acc0, acc1)

    @pl.when(i % 2 == 1)
    def _():
        step(acc1, acc0)


def _mm_ln(x, w, layer, res, g, b, name):
    m, k = x.shape
    n = w.shape[-1]
    tm = MM_LN_TM
    n_tiles = m // tm
    assert x.dtype == BF16

    def done_tile(i):
        return (jnp.maximum(i - 1, 0), 0)

    if isinstance(res, tuple):
        res_split = res[0].shape[0] // tm
        res_specs = [pl.BlockSpec((tm, n), lambda i: (jnp.clip(i - 1, 0, res_split - 1), 0)),
                     pl.BlockSpec((tm, n), lambda i: (jnp.maximum(i - 1 - res_split, 0), 0))]
    else:
        res_split = None
        res_specs = [pl.BlockSpec((tm, n), done_tile)]
        res = (res,)

    return pl.pallas_call(
        functools.partial(_mm_ln_body, layer=layer, res_split=res_split),
        grid=(n_tiles + 1,),
        in_specs=[pl.BlockSpec((tm, k), lambda i: (jnp.minimum(i, n_tiles - 1), 0))] + res_specs + [
                  pl.BlockSpec((1, n), lambda i: (0, 0)),
                  pl.BlockSpec((1, n), lambda i: (0, 0)),
                  pl.BlockSpec(memory_space=pl.ANY)],
        out_specs=pl.BlockSpec((tm, n), done_tile),
        out_shape=jax.ShapeDtypeStruct((m, n), F32),
        scratch_shapes=[pltpu.VMEM((k, n), BF16),
                        pltpu.VMEM((2, W_STAGE_ROWS, n), F32),
                        pltpu.VMEM((tm, n), F32),
                        pltpu.VMEM((tm, n), F32),
                        pltpu.SemaphoreType.DMA((2,))],
        compiler_params=_cparams(1),
        name=name,
    )(x, *res, g, b, w)


def _state_block(i):
    return jnp.maximum(i - NP_STEPS, 0)


def _new_state_block(i):
    return jnp.where(i < NP_STEPS, 0, i - NP_STEPS + 1)


def _mix0_body(p_ref, st_ref, wc_ref, g_ref, b_ref, ws_ref, bs_ref,
               y_ref, c3_ref, vrow_ref, ctx_ref, vprev_ref):
    i = pl.program_id(0)

    z = p_ref[:, 2 * D_A:3 * D_A] * p_ref[:, 0:D_A]

    @pl.when(i == 0)
    def _():
        ctx_ref[0:8, :] = jnp.zeros((8, D_A), F32)
        vprev_ref[...] = jnp.zeros((STEP, D_B), BF16)

    @pl.when(i >= NP_STEPS)
    def _():
        ctx_ref[6:8, :] = _conv_operand(st_ref[...])

    zc = _conv_operand(z)
    ctx_ref[8:8 + STEP, :] = zc
    conv = (wc_ref[0:1, :] * ctx_ref[6:6 + STEP, :]
            + wc_ref[1:2, :] * ctx_ref[7:7 + STEP, :]
            + wc_ref[2:3, :] * zc)
    y_ref[:, 0:D_A] = (p_ref[:, D_A:2 * D_A] * conv).astype(y_ref.dtype)
    c3_ref[...] = z[STEP - 2:STEP, :]
    ctx_ref[0:8, :] = ctx_ref[STEP:STEP + 8, :]

    u = jax.nn.gelu(p_ref[:, 3 * D_A:3 * D_A + D_B])
    v = _ln(jax.nn.gelu(p_ref[:, 3 * D_A + D_B:3 * D_A + 2 * D_B]), g_ref[...], b_ref[...])
    vrow_ref[...] = v
    vb = v.astype(BF16)
    second_chunk = jnp.logical_and(i < NP_STEPS, i % 2 == 1)
    row = lax.broadcasted_iota(jnp.int32, (2 * STEP, 2 * STEP), 0)
    col = lax.broadcasted_iota(jnp.int32, (2 * STEP, 2 * STEP), 1)
    mask = (col // STEP) <= (row // STEP)
    for h in range(SGU_HEADS):
        hs = slice(h * 128, (h + 1) * 128)
        wm = jnp.where(mask, ws_ref[h], 0.0).astype(BF16)
        v_h = vb[:, hs]
        f_first = jnp.dot(wm[0:STEP, 0:STEP], v_h, preferred_element_type=F32) + bs_ref[h, 0:STEP, :]
        f_second = (jnp.dot(wm[STEP:, 0:STEP], vprev_ref[:, hs], preferred_element_type=F32)
                    + jnp.dot(wm[STEP:, STEP:], v_h, preferred_element_type=F32)
                    + bs_ref[h, STEP:, :])
        f = jnp.where(second_chunk, f_second, f_first)
        y_ref[:, D_A + h * 128:D_A + (h + 1) * 128] = (u[:, hs] * f).astype(y_ref.dtype)
    vprev_ref[...] = vb


def _mix0(proj, state_c3, w_conv3, g, b, w_spatial, b_spatial_col):
    n_steps = NP_STEPS + N_STREAMS
    return pl.pallas_call(
        _mix0_body,
        grid=(n_steps,),
        in_specs=[pl.BlockSpec((STEP, 3 * D_A + 2 * D_B), lambda i: (i, 0)),
                  pl.BlockSpec((None, 2, D_A), lambda i: (_state_block(i), 0, 0)),
                  pl.BlockSpec((3, D_A), lambda i: (0, 0)),
                  pl.BlockSpec((1, D_B), lambda i: (0, 0)),
                  pl.BlockSpec((1, D_B), lambda i: (0, 0)),
                  pl.BlockSpec((SGU_HEADS, 128, 128), lambda i: (0, 0, 0)),
                  pl.BlockSpec((SGU_HEADS, 128, 1), lambda i: (0, 0, 0))],
        out_specs=[pl.BlockSpec((STEP, D), lambda i: (i, 0)),
                   pl.BlockSpec((None, 2, D_A), lambda i: (_new_state_block(i), 0, 0)),
                   pl.BlockSpec((None, STEP, D_B), lambda i: (_state_block(i), 0, 0))],
        out_shape=[jax.ShapeDtypeStruct((M_TOK, D), BF16),
                   jax.ShapeDtypeStruct((N_STREAMS + 1, 2, D_A), F32),
                   jax.ShapeDtypeStruct((N_STREAMS, STEP, D_B), F32)],
        scratch_shapes=[pltpu.VMEM((STEP + 8, D_A), F32),
                        pltpu.VMEM((STEP, D_B), BF16)],
        compiler_params=_cparams(1),
        name="mix_even",
    )(proj, state_c3, w_conv3, g, b, w_spatial, b_spatial_col)


CONF_W = 31
CTX_PAD = 32
CONV_COLS = 512
SUBLANES = 8
SHIFT_ROWS = CTX_PAD + STEP - SUBLANES


def _mix1_body(p_ref, st_ref, w_ref, bdw_ref, g_ref, b_ref, y_ref, c31_ref, ctx_ref, cv_ref, sh_ref):
    i = pl.program_id(0)
    glu = p_ref[:, 0:D] * jax.nn.sigmoid(p_ref[:, D:2 * D])

    @pl.when(i == 0)
    def _():
        ctx_ref[0:CTX_PAD, :] = jnp.zeros((CTX_PAD, D), F32)

    @pl.when(i >= NP_STEPS)
    def _():
        ctx_ref[CTX_PAD - (CONF_W - 1):CTX_PAD, :] = _conv_operand(st_ref[...])

    ctx_ref[CTX_PAD:CTX_PAD + STEP, :] = _conv_operand(glu)
    c31_ref[...] = glu[STEP - (CONF_W - 1):STEP, :]
    first = CTX_PAD - (CONF_W - 1)
    for c in range(D // CONV_COLS):
        cs = slice(c * CONV_COLS, (c + 1) * CONV_COLS)
        for s in range(1, SUBLANES):
            sh_ref[s - 1, :, :] = ctx_ref[s:s + SHIFT_ROWS, cs]
        acc = None
        for k in range(CONF_W):
            q, s = divmod(first + k, SUBLANES)
            rows = slice(q * SUBLANES, q * SUBLANES + STEP)
            operand = ctx_ref[rows, cs] if s == 0 else sh_ref[s - 1, rows, :]
            term = w_ref[k:k + 1, cs] * operand
            acc = term if acc is None else acc + term
        cv_ref[:, cs] = acc + bdw_ref[:, cs]
    c = _ln(cv_ref[...], g_ref[...], b_ref[...])
    y_ref[...] = (c * jax.nn.sigmoid(c)).astype(y_ref.dtype)
    ctx_ref[0:CTX_PAD, :] = ctx_ref[STEP:STEP + CTX_PAD, :]


def _mix1(proj, state_c31, w_dw, b_dw, g, b):
    n_steps = NP_STEPS + N_STREAMS
    return pl.pallas_call(
        _mix1_body,
        grid=(n_steps,),
        in_specs=[pl.BlockSpec((STEP, 2 * D), lambda i: (i, 0)),
                  pl.BlockSpec((None, CONF_W - 1, D), lambda i: (_state_block(i), 0, 0)),
                  pl.BlockSpec((CONF_W, D), lambda i: (0, 0)),
                  pl.BlockSpec((1, D), lambda i: (0, 0)),
                  pl.BlockSpec((1, D), lambda i: (0, 0)),
                  pl.BlockSpec((1, D), lambda i: (0, 0))],
        out_specs=[pl.BlockSpec((STEP, D), lambda i: (i, 0)),
                   pl.BlockSpec((None, CONF_W - 1, D), lambda i: (_new_state_block(i), 0, 0))],
        out_shape=[jax.ShapeDtypeStruct((M_TOK, D), BF16),
                   jax.ShapeDtypeStruct((N_STREAMS + 1, CONF_W - 1, D), F32)],
        scratch_shapes=[pltpu.VMEM((CTX_PAD + STEP, D), F32),
                        pltpu.VMEM((STEP, D), F32),
                        pltpu.VMEM((SUBLANES - 1, SHIFT_ROWS, CONV_COLS), F32)],
        compiler_params=_cparams(1),
        name="mix_odd",
    )(proj, state_c31, w_dw, b_dw, g, b)


ATTN_TM_PROMPT = 512
ATTN_TM_SAMPLE = 128


def _attn_block_body(x_ref, k_ref, v_ref, g_ref, b_ref, wq_hbm, wo_hbm, *rest,
                     layer, n_tiles, streams):
    o_ref, wq_b, wo_b, stage, acc0, acc1, x_prev, q_scr, a_scr, sem = rest[-10:]
    i = pl.program_id(0)
    rows_per_stream = x_ref.shape[0] // streams

    @pl.when(i == 0)
    def _():
        acc1[...] = jnp.zeros_like(acc1)
        x_prev[...] = jnp.zeros_like(x_prev)
        _load_weight_bf16(wq_hbm.at[layer], wq_b, stage, sem)
        _load_weight_bf16(wo_hbm.at[layer], wo_b, stage, sem)

    def step(acc_new, acc_old):
        x = x_ref[...]
        q_scr[...] = jnp.dot(x.astype(BF16), wq_b[...], preferred_element_type=F32).astype(BF16)
        for s in range(streams):
            rows = slice(s * rows_per_stream, (s + 1) * rows_per_stream)
            for h in range(HEADS):
                hs = slice(h * HEAD_DIM, (h + 1) * HEAD_DIM)
                sc = lax.dot_general(q_scr[rows, hs], k_ref[s, :, hs],
                                     (((1,), (1,)), ((), ())), preferred_element_type=F32) * ATTN_SCALE
                p = jnp.exp(sc - jnp.max(sc, axis=-1, keepdims=True))
                p = p / jnp.sum(p, axis=-1, keepdims=True)
                a_scr[rows, hs] = jnp.dot(p.astype(BF16), v_ref[s, :, hs],
                                          preferred_element_type=F32).astype(BF16)
        acc_new[...] = jnp.dot(a_scr[...], wo_b[...], preferred_element_type=F32)
        o_ref[...] = _ln(ALPHA * x_prev[...] + acc_old[...], g_ref[...], b_ref[...])
        x_prev[...] = x

    @pl.when(jnp.logical_and(i <= n_tiles, i % 2 == 0))
    def _():
        step(acc0, acc1)

    @pl.when(jnp.logical_and(i <= n_tiles, i % 2 == 1))
    def _():
        step(acc1, acc0)

    @pl.when(i > n_tiles)
    def _():
        o_ref[...] = jnp.zeros_like(o_ref)


def _attn_block_call(x, k, v, g, b, w_q, w_o, layer, *, tm, streams, first_tile, n_tiles, n_clear,
                     prev_out, name):
    last = n_tiles - 1

    def in_tile(i):
        return (first_tile + jnp.minimum(i, last), 0)

    def kv_block(i):
        return (layer, jnp.minimum(i, last) if k.shape[1] > streams else 0, 0, 0)

    def out_tile(i):
        return (first_tile + jnp.maximum(i - 1, 0), 0)

    in_specs = [pl.BlockSpec((tm, D), in_tile),
                pl.BlockSpec((None, streams, N_MEM, D), kv_block),
                pl.BlockSpec((None, streams, N_MEM, D), kv_block),
                pl.BlockSpec((1, D), lambda i: (0, 0)),
                pl.BlockSpec((1, D), lambda i: (0, 0)),
                pl.BlockSpec(memory_space=pl.ANY),
                pl.BlockSpec(memory_space=pl.ANY)]
    args = [x, k, v, g, b, w_q, w_o]
    aliases = {}
    if prev_out is not None:
        in_specs.append(pl.BlockSpec(memory_space=pl.ANY))
        args.append(prev_out)
        aliases = {len(args) - 1: 0}
    return pl.pallas_call(
        functools.partial(_attn_block_body, layer=layer, n_tiles=n_tiles, streams=streams),
        grid=(n_tiles + 1 + n_clear,),
        in_specs=in_specs,
        out_specs=pl.BlockSpec((tm, D), out_tile),
        out_shape=jax.ShapeDtypeStruct((M_TOK, D), F32),
        scratch_shapes=[pltpu.VMEM((D, D), BF16),
                        pltpu.VMEM((D, D), BF16),
                        pltpu.VMEM((2, W_STAGE_ROWS, D), F32),
                        pltpu.VMEM((tm, D), F32),
                        pltpu.VMEM((tm, D), F32),
                        pltpu.VMEM((tm, D), F32),
                        pltpu.VMEM((tm, D), BF16),
                        pltpu.VMEM((tm, D), BF16),
                        pltpu.SemaphoreType.DMA((2,))],
        input_output_aliases=aliases,
        compiler_params=_cparams(1),
        name=name,
    )(*args)


def _attn_block(x, pk, pv, ck, cv, g, b, w_q, w_o, layer):
    n_sample = N_STREAMS * STEP
    out = _attn_block_call(x, pk, pv, g, b, w_q, w_o, layer, tm=ATTN_TM_PROMPT, streams=1,
                           first_tile=0, n_tiles=N_PROMPT // ATTN_TM_PROMPT,
                           n_clear=n_sample // ATTN_TM_PROMPT, prev_out=None, name="attn_prompt")
    return _attn_block_call(x, ck, cv, g, b, w_q, w_o, layer, tm=ATTN_TM_SAMPLE,
                            streams=ATTN_TM_SAMPLE // STEP, first_tile=N_PROMPT // ATTN_TM_SAMPLE,
                            n_tiles=n_sample // ATTN_TM_SAMPLE, n_clear=0, prev_out=out,
                            name="attn_sample")


TOP_K = 2
ROUTE_TM = 512
EXP_TM = 512
N_ASSIGN = M_TOK * TOP_K
N_EXP_TILES = N_ASSIGN // EXP_TM
N_VISITS = N_EXP_TILES + N_EXPERTS - 1
MOVE_TM = 256
LANE = 128
CHUNKS = D // LANE
ISSUE_UNROLL = 8


def _to_chunk_major(dst_ref, x, rows):
    for c in range(CHUNKS):
        dst_ref[pl.ds(c, rows, stride=CHUNKS), :] = x[:, c * LANE:(c + 1) * LANE]


def _from_chunk_major(src_ref, rows):
    return jnp.concatenate([src_ref[pl.ds(c, rows, stride=CHUNKS), :] for c in range(CHUNKS)], axis=-1)


def _route_body(x_ref, wr_ref, br_ref, idx_ref, wts_ref, cnt_ref, run_ref):
    i = pl.program_id(0)

    @pl.when(i == 0)
    def _():
        run_ref[...] = jnp.zeros_like(run_ref)

    logits = _bdot(x_ref[...], wr_ref[...]) + br_ref[...]
    lane = lax.broadcasted_iota(jnp.int32, logits.shape, 1)

    def first_argmax(vals):
        top = jnp.max(vals, axis=-1, keepdims=True)
        idx = jnp.min(jnp.where(vals == top, lane, ROUTE_LANES), axis=-1, keepdims=True)
        return top, idx

    g_logits = jnp.where(lane < N_GROUPS, logits, NEG)
    g_top, g_idx = first_argmax(g_logits)
    p_group = 1.0 / jnp.sum(jnp.exp(g_logits - g_top), axis=-1, keepdims=True)
    lo_lane = N_GROUPS + E_PER * g_idx
    e_logits = jnp.where(jnp.logical_and(lane >= lo_lane, lane < lo_lane + E_PER), logits, NEG)
    v1, i1 = first_argmax(e_logits)
    v2, i2 = first_argmax(jnp.where(lane == i1, NEG, e_logits))
    t = jnp.exp(v2 - v1)
    w1 = p_group / (1.0 + t)
    w2 = p_group * t / (1.0 + t)

    a1 = (lane == i1).astype(F32)
    a2 = (lane == i2).astype(F32)
    a = a1 + a2
    tm = logits.shape[0]
    earlier = (lax.broadcasted_iota(jnp.int32, (tm, tm), 1)
               < lax.broadcasted_iota(jnp.int32, (tm, tm), 0)).astype(BF16)
    before = jnp.dot(earlier, a.astype(BF16), preferred_element_type=F32) + run_ref[...]
    r1 = jnp.sum(a1 * before, axis=-1, keepdims=True)
    r2 = jnp.sum(a2 * before, axis=-1, keepdims=True)
    run_ref[...] += jnp.sum(a, axis=0, keepdims=True)
    cnt_ref[...] = run_ref[...].astype(jnp.int32)

    e1 = (i1 - N_GROUPS).astype(F32)
    e2 = (i2 - N_GROUPS).astype(F32)
    packed = (jnp.where(lane == 0, e1, 0.0) + jnp.where(lane == 1, e2, 0.0)
              + jnp.where(lane == 2, r1, 0.0) + jnp.where(lane == 3, r2, 0.0))
    idx_ref[...] = packed.T[0:8, :].astype(jnp.int32)
    wts_ref[...] = jnp.where(lane == 0, w1, 0.0) + jnp.where(lane == 1, w2, 0.0)


def _route(x, wr, br):
    m = x.shape[0]
    tm = ROUTE_TM
    return pl.pallas_call(
        _route_body,
        grid=(m // tm,),
        in_specs=[pl.BlockSpec((tm, D), lambda i: (i, 0)),
                  pl.BlockSpec((D, ROUTE_LANES), lambda i: (0, 0)),
                  pl.BlockSpec((1, ROUTE_LANES), lambda i: (0, 0))],
        out_specs=[pl.BlockSpec((8, tm), lambda i: (0, i)),
                   pl.BlockSpec((tm, ROUTE_LANES), lambda i: (i, 0)),
                   pl.BlockSpec((1, ROUTE_LANES), lambda i: (0, 0))],
        out_shape=[jax.ShapeDtypeStruct((8, m), jnp.int32),
                   jax.ShapeDtypeStruct((m, ROUTE_LANES), F32),
                   jax.ShapeDtypeStruct((1, ROUTE_LANES), jnp.int32)],
        scratch_shapes=[pltpu.VMEM((1, ROUTE_LANES), F32)],
        compiler_params=_cparams(1),
        name="moe_route",
    )(x, wr, br)


def _plan_body(cnt_ref, idx_ref, pos_ref, vt_ref, ve_ref, vlo_ref, vhi_ref, nvis_ref):
    shift = EXP_TM.bit_length() - 1
    e1 = idx_ref[0:1, :]
    e2 = idx_ref[1:2, :]
    pos1 = idx_ref[2:3, :]
    pos2 = idx_ref[3:4, :]
    start = jnp.int32(0)
    n_vis = jnp.int32(0)
    last_e = jnp.int32(0)
    for e in range(N_EXPERTS):
        cnt = cnt_ref[N_GROUPS + e]
        end = start + cnt
        pos1 = pos1 + jnp.where(e1 == e, start, 0)
        pos2 = pos2 + jnp.where(e2 == e, start, 0)
        first_tile = lax.shift_right_logical(start, shift)
        n_tiles = jnp.where(cnt > 0, lax.shift_right_logical(end + (EXP_TM - 1), shift) - first_tile, 0)

        def add_visit(k, carry, e=e, start=start, end=end, first_tile=first_tile, n_vis=n_vis):
            tile = first_tile + k
            vt_ref[n_vis + k] = tile
            ve_ref[n_vis + k] = e
            vlo_ref[n_vis + k] = jnp.maximum(start - tile * EXP_TM, 0)
            vhi_ref[n_vis + k] = jnp.minimum(end - tile * EXP_TM, EXP_TM)
            return carry

        lax.fori_loop(0, n_tiles, add_visit, 0)
        n_vis = n_vis + n_tiles
        last_e = jnp.where(cnt > 0, e, last_e)
        start = end
    nvis_ref[0] = n_vis

    def add_idle(k, carry):
        vt_ref[k] = N_EXP_TILES - 1
        ve_ref[k] = last_e
        vlo_ref[k] = 0
        vhi_ref[k] = 0
        return carry

    lax.fori_loop(n_vis, N_VISITS, add_idle, 0)
    pos_ref[...] = jnp.concatenate([pos1, pos2, jnp.zeros((6, M_TOK), jnp.int32)], axis=0)


def _plan(cnt, idx):
    smem = pl.BlockSpec(memory_space=pltpu.SMEM)
    vmem = pl.BlockSpec(memory_space=pltpu.VMEM)
    table = jax.ShapeDtypeStruct((N_VISITS,), jnp.int32)
    return pl.pallas_call(
        _plan_body,
        in_specs=[smem, vmem],
        out_specs=[vmem, smem, smem, smem, smem, smem],
        out_shape=[jax.ShapeDtypeStruct((8, M_TOK), jnp.int32), table, table, table, table,
                   jax.ShapeDtypeStruct((1,), jnp.int32)],
        name="moe_plan",
    )(cnt, idx)


def _row_copy(src_ref, src_row, dst_ref, dst_row, sem):
    return pltpu.make_async_copy(src_ref.at[pl.ds(src_row * CHUNKS, CHUNKS)],
                                 dst_ref.at[pl.ds(dst_row * CHUNKS, CHUNKS)], sem)


def _scatter_body(pos1_ref, pos2_ref, x_ref, xs_hbm, stage, sem):
    i = pl.program_id(0)
    last = pl.num_programs(0) - 1
    slot = i % 2

    def wait_slot(s):
        for _ in range(TOP_K):
            pltpu.make_async_copy(stage.at[s], xs_hbm.at[pl.ds(0, MOVE_TM * CHUNKS)], sem.at[s]).wait()

    @pl.when(i >= 2)
    def _():
        wait_slot(slot)

    _to_chunk_major(stage.at[slot], x_ref[...], MOVE_TM)

    def issue(r, carry):
        t = i * MOVE_TM + r
        _row_copy(stage.at[slot], r, xs_hbm, pos1_ref[t], sem.at[slot]).start()
        _row_copy(stage.at[slot], r, xs_hbm, pos2_ref[t], sem.at[slot]).start()
        return carry

    lax.fori_loop(0, MOVE_TM, issue, 0, unroll=ISSUE_UNROLL)

    @pl.when(i == last)
    def _():
        wait_slot(1 - slot)
        wait_slot(slot)


def _scatter(pos1, pos2, x):
    m = x.shape[0]
    assert (m // MOVE_TM) >= 2
    grid_spec = pltpu.PrefetchScalarGridSpec(
        num_scalar_prefetch=2,
        grid=(m // MOVE_TM,),
        in_specs=[pl.BlockSpec((MOVE_TM, D), lambda i, p1, p2: (i, 0))],
        out_specs=pl.BlockSpec(memory_space=pl.ANY),
        scratch_shapes=[pltpu.VMEM((2, MOVE_TM * CHUNKS, LANE), F32),
                        pltpu.SemaphoreType.DMA((2,))])
    return pl.pallas_call(
        _scatter_body,
        grid_spec=grid_spec,
        out_shape=jax.ShapeDtypeStruct((N_ASSIGN * CHUNKS, LANE), F32),
        compiler_params=_cparams(1),
        name="moe_scatter",
    )(pos1, pos2, x)


def _experts_body(vt_ref, ve_ref, vlo_ref, vhi_ref, nvis_ref, x_ref, wg_ref, wu_ref, wd_ref,
                  o_ref, wg_b, wu_b, wd_b):
    v = pl.program_id(0)
    prev = jnp.maximum(v - 1, 0)

    @pl.when(v < nvis_ref[0])
    def _():
        @pl.when(jnp.logical_or(v == 0, ve_ref[v] != ve_ref[prev]))
        def _():
            wg_b[...] = wg_ref[...].astype(BF16)
            wu_b[...] = wu_ref[...].astype(BF16)
            wd_b[...] = wd_ref[...].astype(BF16)

        xb = _from_chunk_major(x_ref, EXP_TM).astype(BF16)
        gate = jnp.dot(xb, wg_b[...], preferred_element_type=F32)
        up = jnp.dot(xb, wu_b[...], preferred_element_type=F32)
        h = (gate * jax.nn.sigmoid(gate) * up).astype(BF16)
        y = jnp.dot(h, wd_b[...], preferred_element_type=F32)
        row = lax.broadcasted_iota(jnp.int32, (EXP_TM, 1), 0)
        mine = jnp.logical_and(row >= vlo_ref[v], row < vhi_ref[v])
        first_visit = jnp.logical_or(v == 0, vt_ref[v] != vt_ref[prev])

        @pl.when(first_visit)
        def _():
            _to_chunk_major(o_ref, jnp.where(mine, y, 0.0), EXP_TM)

        @pl.when(jnp.logical_not(first_visit))
        def _():
            _to_chunk_major(o_ref, jnp.where(mine, y, _from_chunk_major(o_ref, EXP_TM)), EXP_TM)


def _experts(vt, ve, vlo, vhi, nvis, xs, w_gate, w_up, w_down, layer):
    def tile_map(v, vt, ve, vlo, vhi, nvis):
        return (vt[v], 0)

    def weight_map(v, vt, ve, vlo, vhi, nvis):
        return (layer, ve[v], 0, 0)

    grid_spec = pltpu.PrefetchScalarGridSpec(
        num_scalar_prefetch=5,
        grid=(N_VISITS,),
        in_specs=[pl.BlockSpec((EXP_TM * CHUNKS, LANE), tile_map),
                  pl.BlockSpec((None, None, D, D_EXPERT), weight_map),
                  pl.BlockSpec((None, None, D, D_EXPERT), weight_map),
                  pl.BlockSpec((None, None, D_EXPERT, D), weight_map)],
        out_specs=pl.BlockSpec((EXP_TM * CHUNKS, LANE), tile_map),
        scratch_shapes=[pltpu.VMEM((D, D_EXPERT), BF16),
                        pltpu.VMEM((D, D_EXPERT), BF16),
                        pltpu.VMEM((D_EXPERT, D), BF16)])
    return pl.pallas_call(
        _experts_body,
        grid_spec=grid_spec,
        out_shape=jax.ShapeDtypeStruct((N_ASSIGN * CHUNKS, LANE), F32),
        compiler_params=_cparams(1),
        name="moe_experts",
    )(vt, ve, vlo, vhi, nvis, xs, w_gate, w_up, w_down)


def _combine_body(pos1_ref, pos2_ref, x_ref, wts_ref, g_ref, b_ref, ys_hbm, *rest, split):
    out_refs, (ybuf, sem) = rest[:-2], rest[-2:]
    i = pl.program_id(0)
    slot = i % 2

    def gather_rows(step, s):
        def issue(r, carry):
            t = step * MOVE_TM + r
            _row_copy(ys_hbm, pos1_ref[t], ybuf.at[s, 0], r, sem.at[s]).start()
            _row_copy(ys_hbm, pos2_ref[t], ybuf.at[s, 1], r, sem.at[s]).start()
            return carry
        lax.fori_loop(0, MOVE_TM, issue, 0, unroll=ISSUE_UNROLL)

    @pl.when(i == 0)
    def _():
        gather_rows(0, 0)

    @pl.when(i + 1 < pl.num_programs(0))
    def _():
        gather_rows(i + 1, 1 - slot)

    for k in range(TOP_K):
        pltpu.make_async_copy(ys_hbm.at[pl.ds(0, MOVE_TM * CHUNKS)], ybuf.at[slot, k], sem.at[slot]).wait()
    y = (wts_ref[:, 0:1] * _from_chunk_major(ybuf.at[slot, 0], MOVE_TM)
         + wts_ref[:, 1:2] * _from_chunk_major(ybuf.at[slot, 1], MOVE_TM))
    out = _ln(ALPHA * x_ref[...] + y, g_ref[...], b_ref[...])
    if split is None:
        out_refs[0][...] = out
    else:
        @pl.when(i < split)
        def _():
            out_refs[0][...] = out

        @pl.when(i >= split)
        def _():
            out_refs[1][...] = out


def _combine(pos1, pos2, x, wts, g, b, ys, split_rows=None):
    m = x.shape[0]
    if split_rows is None:
        split = None
        out_specs = pl.BlockSpec((MOVE_TM, D), lambda i, p1, p2: (i, 0))
        out_shape = jax.ShapeDtypeStruct((m, D), F32)
    else:
        split = split_rows // MOVE_TM
        out_specs = [pl.BlockSpec((MOVE_TM, D), lambda i, p1, p2: (jnp.minimum(i, split - 1), 0)),
                     pl.BlockSpec((MOVE_TM, D), lambda i, p1, p2: (jnp.maximum(i - split, 0), 0))]
        out_shape = [jax.ShapeDtypeStruct((split_rows, D), F32),
                     jax.ShapeDtypeStruct((m - split_rows, D), F32)]
    grid_spec = pltpu.PrefetchScalarGridSpec(
        num_scalar_prefetch=2,
        grid=(m // MOVE_TM,),
        in_specs=[pl.BlockSpec((MOVE_TM, D), lambda i, p1, p2: (i, 0)),
                  pl.BlockSpec((MOVE_TM, ROUTE_LANES), lambda i, p1, p2: (i, 0)),
                  pl.BlockSpec((1, D), lambda i, p1, p2: (0, 0)),
                  pl.BlockSpec((1, D), lambda i, p1, p2: (0, 0)),
                  pl.BlockSpec(memory_space=pl.ANY)],
        out_specs=out_specs,
        scratch_shapes=[pltpu.VMEM((2, TOP_K, MOVE_TM * CHUNKS, LANE), F32),
                        pltpu.SemaphoreType.DMA((2,))])
    return pl.pallas_call(
        functools.partial(_combine_body, split=split),
        grid_spec=grid_spec,
        out_shape=out_shape,
        compiler_params=_cparams(1),
        name="moe_combine",
    )(pos1, pos2, x, wts, g, b, ys)


def _moe(x, wr, br, w_gate, w_up, w_down, layer, g, b, split_rows=None):
    idx, wts, cnt = _route(x, wr, br)
    pos, vt, ve, vlo, vhi, nvis = _plan(cnt.reshape(ROUTE_LANES), idx)
    xs = _scatter(pos[0], pos[1], x)
    ys = _experts(vt, ve, vlo, vhi, nvis, xs, w_gate, w_up, w_down, layer)
    return _combine(pos[0], pos[1], x, wts, g, b, ys, split_rows)


def _router_params(w_rg, b_rg, w_re, b_re):
    w = jnp.concatenate([w_rg, jnp.transpose(w_re, (1, 0, 2)).reshape(D, N_EXPERTS)], axis=1)
    bias = jnp.concatenate([b_rg, b_re.reshape(N_EXPERTS)])
    pad = ROUTE_LANES - w.shape[1]
    return jnp.pad(w, ((0, 0), (0, pad))), jnp.pad(bias, (0, pad))[None, :]


def kernel(x_prompt, x_sample, mem_prompt, cache_mem_k, cache_mem_v, state_conv3, state_conv31,
           ln_gain, ln_bias, w_in_ab, w_conv3, ln_sgu_gain, ln_sgu_bias, w_spatial, b_spatial, w_out_ab,
           w_pw1, w_dw31, b_dw31, ln_conf_gain, ln_conf_bias, w_pw2,
           w_mem_q, w_mem_k, w_mem_v, w_mem_o,
           w_route_group, b_route_group, w_route_expert, b_route_expert,
           w_exp_gate, w_exp_up, w_exp_down):
    depth = ln_gain.shape[0]
    x = (x_prompt.reshape(N_PROMPT, D), x_sample.reshape(N_STREAMS * STEP, D))

    mem = mem_prompt.reshape(N_MEM, D)
    pk, pk_b = _mem_kv(mem, w_mem_k, "mem_k")
    pv, pv_b = _mem_kv(mem, w_mem_v, "mem_v")
    pk_b = pk_b.reshape(depth, 1, N_MEM, D)
    pv_b = pv_b.reshape(depth, 1, N_MEM, D)
    ck = cache_mem_k.astype(BF16).reshape(depth, N_STREAMS, N_MEM, D)
    cv = cache_mem_v.astype(BF16).reshape(depth, N_STREAMS, N_MEM, D)

    def ln_params(l, k):
        return ln_gain[l, k][None, :], ln_bias[l, k][None, :]

    c3_new = c31_new = v_rows = None
    for l in range(depth):
        i = l // 2
        if l % 2 == 0:
            if isinstance(x, tuple):
                proj = _mm_two(x[0], x[1], w_in_ab, i, F32, 1024, 512, "proj_in_ab")
            else:
                proj = _mm(x, w_in_ab, i, F32, 1024, 512, "proj_in_ab")
            y_mix, c3_new, v_rows = _mix0(
                proj, state_conv3[i], w_conv3[i], ln_sgu_gain[i][None, :], ln_sgu_bias[i][None, :],
                w_spatial[i], b_spatial[i][:, :, None])
            w_out = w_out_ab
        else:
            proj = _mm(x, w_pw1, i, F32, 1024, 512, "proj_pw1")
            y_mix, c31_new = _mix1(
                proj, state_conv31[i], w_dw31[i], b_dw31[i][None, :],
                ln_conf_gain[i][None, :], ln_conf_bias[i][None, :])
            w_out = w_pw2
        x = _mm_ln(y_mix, w_out, i, x, *ln_params(l, 0), name="mix_out_ln")
        x = _attn_block(x, pk_b, pv_b, ck, cv, *ln_params(l, 1), w_mem_q, w_mem_o, l)
        wr, br = _router_params(w_route_group[l], b_route_group[l], w_route_expert[l], b_route_expert[l])
        x = _moe(x, wr, br, w_exp_gate, w_exp_up, w_exp_down, l, *ln_params(l, 2),
                 split_rows=N_PROMPT if l == depth - 1 else None)

    y_prompt = x[0].reshape(1, N_PROMPT, D)
    y_sample = x[1].reshape(N_STREAMS, STEP, D)
    prompt_mem_k = pk.reshape(depth, 1, N_MEM, HEADS, HEAD_DIM)
    prompt_mem_v = pv.reshape(depth, 1, N_MEM, HEADS, HEAD_DIM)
    prompt_conv3 = c3_new[0].reshape(1, 1, 2, D_A)
    sample_conv3 = c3_new[1:].reshape(1, N_STREAMS, 2, D_A)
    prompt_conv31 = c31_new[0].reshape(1, 1, CONF_W - 1, D)
    sample_conv31 = c31_new[1:].reshape(1, N_STREAMS, CONF_W - 1, D)
    sample_sgu_v = v_rows.reshape(1, N_STREAMS, STEP, D_B)
    return (y_prompt, y_sample, prompt_mem_k, prompt_mem_v, prompt_conv3, prompt_conv31,
            sample_conv3, sample_conv31, sample_sgu_v)
```

```python
import functools

import jax
import jax.numpy as jnp
from jax import lax
from jax.experimental import pallas as pl
from jax.experimental.pallas import tpu as pltpu

F32 = jnp.float32
BF16 = jnp.bfloat16

D = 2048
N_PROMPT = 8192
N_STREAMS = 16
STEP = 64
NP_STEPS = N_PROMPT // STEP
M_TOK = N_PROMPT + N_STREAMS * STEP
N_MEM = 256
HEADS = 4
HEAD_DIM = D // HEADS
D_A = 1024
D_B = 1024
SGU_HEADS = 8
N_GROUPS = 4
E_PER = 4
N_EXPERTS = 16
D_EXPERT = 512
LN_EPS = 1e-5
ALPHA = (2.0 * 2) ** 0.25
ATTN_SCALE = HEAD_DIM ** -0.5
ROUTE_LANES = 128
NEG = -1e30

VMEM_LIMIT_BYTES = 60000 * 1024


def _cparams(n_axes):
    return pltpu.CompilerParams(
        dimension_semantics=("arbitrary",) * n_axes,
        vmem_limit_bytes=VMEM_LIMIT_BYTES)


def _ln(x, g, b):
    mu = jnp.mean(x, axis=-1, keepdims=True)
    xc = x - mu
    var = jnp.mean(xc * xc, axis=-1, keepdims=True)
    return xc * lax.rsqrt(var + LN_EPS) * g + b


def _bdot(a, b):
    return jnp.dot(a.astype(BF16), b.astype(BF16), preferred_element_type=F32)


def _conv_operand(x):
    return x.astype(BF16).astype(F32)


def _mem_kv_body(x_ref, w_ref, o_ref, ob_ref):
    y = _bdot(x_ref[...], w_ref[...])
    o_ref[...] = y
    ob_ref[...] = y.astype(BF16)


def _mem_kv(mem, w, name):
    depth, k, n = w.shape
    tn = 512
    out_block = pl.BlockSpec((None, N_MEM, tn), lambda l, j: (l, 0, j))
    return pl.pallas_call(
        _mem_kv_body,
        grid=(depth, n // tn),
        in_specs=[pl.BlockSpec((N_MEM, k), lambda l, j: (0, 0)),
                  pl.BlockSpec((None, k, tn), lambda l, j: (l, 0, j))],
        out_specs=[out_block, out_block],
        out_shape=[jax.ShapeDtypeStruct((depth, N_MEM, n), F32),
                   jax.ShapeDtypeStruct((depth, N_MEM, n), BF16)],
        compiler_params=_cparams(2),
        name=name,
    )(mem, w)


MM_LN_TM = 512
W_STAGE_ROWS = 128


def _load_weight_bf16(w_hbm, wb, stage, sem):
    rows = stage.shape[1]
    n_chunks = wb.shape[0] // rows

    def copy(k):
        return pltpu.make_async_copy(w_hbm.at[pl.ds(k * rows, rows)], stage.at[k % 2], sem.at[k % 2])

    copy(0).start()
    for k in range(n_chunks):
        if k + 1 < n_chunks:
            copy(k + 1).start()
        copy(k).wait()
        wb[k * rows:(k + 1) * rows, :] = stage[k % 2].astype(BF16)


def _mm_ln_body(x_ref, *rest, layer, res_split):
    if res_split is None:
        r_ref, g_ref, b_ref, w_hbm, o_ref, wb, stage, acc0, acc1, sem = rest
    else:
        ra_ref, rb_ref, g_ref, b_ref, w_hbm, o_ref, wb, stage, acc0, acc1, sem = rest
    i = pl.program_id(0)

    @pl.when(i == 0)
    def _():
        acc1[...] = jnp.zeros_like(acc1)
        _load_weight_bf16(w_hbm.at[layer], wb, stage, sem)

    def step(acc_new, acc_old):
        acc_new[...] = jnp.dot(x_ref[...], wb[...], preferred_element_type=F32)
        if res_split is None:
            res = r_ref[...]
        else:
            res = jnp.where(i - 1 < res_split, ra_ref[...], rb_ref[...])
        o_ref[...] = _ln(ALPHA * res + acc_old[...], g_ref[...], b_ref[...])

    @pl.when(i % 2 == 0)
    def _():
        step(acc0, acc1)

    @pl.when(i % 2 == 1)
    def _():
        step(acc1, acc0)


def _mm_ln(x, w, layer, res, g, b, name):
    m, k = x.shape
    n = w.shape[-1]
    tm = MM_LN_TM
    n_tiles = m // tm
    assert x.dtype == BF16

    def done_tile(i):
        return (jnp.maximum(i - 1, 0), 0)

    if isinstance(res, tuple):
        res_split = res[0].shape[0] // tm
        res_specs = [pl.BlockSpec((tm, n), lambda i: (jnp.clip(i - 1, 0, res_split - 1), 0)),
                     pl.BlockSpec((tm, n), lambda i: (jnp.maximum(i - 1 - res_split, 0), 0))]
    else:
        res_split = None
        res_specs = [pl.BlockSpec((tm, n), done_tile)]
        res = (res,)

    return pl.pallas_call(
        functools.partial(_mm_ln_body, layer=layer, res_split=res_split),
        grid=(n_tiles + 1,),
        in_specs=[pl.BlockSpec((tm, k), lambda i: (jnp.minimum(i, n_tiles - 1), 0))] + res_specs + [
                  pl.BlockSpec((1, n), lambda i: (0, 0)),
                  pl.BlockSpec((1, n), lambda i: (0, 0)),
                  pl.BlockSpec(memory_space=pl.ANY)],
        out_specs=pl.BlockSpec((tm, n), done_tile),
        out_shape=jax.ShapeDtypeStruct((m, n), F32),
        scratch_shapes=[pltpu.VMEM((k, n), BF16),
                        pltpu.VMEM((2, W_STAGE_ROWS, n), F32),
                        pltpu.VMEM((tm, n), F32),
                        pltpu.VMEM((tm, n), F32),
                        pltpu.SemaphoreType.DMA((2,))],
        compiler_params=_cparams(1),
        name=name,
    )(x, *res, g, b, w)


MIX_TM = 256
MIX_SUB = MIX_TM // STEP
N_PROMPT_TILES = N_PROMPT // MIX_TM
N_MIX_TILES = M_TOK // MIX_TM
N_SAMPLE_TILES = N_MIX_TILES - N_PROMPT_TILES


def _mix_block_body(*refs, n_x, n_params, n_outs, layer, prepare, mixer):
    x_refs, refs = refs[:n_x], refs[n_x:]
    st_ref, refs = refs[0], refs[1:]
    params, refs = refs[:n_params], refs[n_params:]
    w_hbm, refs = refs[0], refs[1:]
    outs, refs = refs[:n_outs], refs[n_outs:]
    wb, stage, p0, p1, sem = refs[:5]
    scratch = refs[5:]
    i = pl.program_id(0)

    @pl.when(i == 0)
    def _():
        p1[...] = jnp.zeros_like(p1)
        for ref in tuple(scratch) + tuple(outs[1:2]):
            ref[...] = jnp.zeros_like(ref)
        _load_weight_bf16(w_hbm.at[layer], wb, stage, sem)

    def step(p_new, p_old):
        if n_x == 1:
            x = x_refs[0][...]
        else:
            x = jnp.where(i < N_PROMPT_TILES, x_refs[0][...], x_refs[1][...])
        p_new[...] = jnp.dot(x.astype(BF16), wb[...], preferred_element_type=F32)
        is_sample = i - 1 >= N_PROMPT_TILES
        pre = prepare(params)
        for j in range(MIX_SUB):
            mixer(p_old, j, is_sample, st_ref, params, outs, scratch, pre)

    @pl.when(i % 2 == 0)
    def _():
        step(p0, p1)

    @pl.when(i % 2 == 1)
    def _():
        step(p1, p0)


def _mixed_tile(i):
    return jnp.maximum(i - 1, 0)


def _sample_tile(i):
    return jnp.clip(i - 1 - N_PROMPT_TILES, 0, N_SAMPLE_TILES - 1)


def _mix_block(x, w, layer, state, params, param_specs, out_shapes, out_specs, scratch_shapes,
               prepare, mixer, name):
    k, n = w.shape[-2:]
    last = N_MIX_TILES - 1
    if isinstance(x, tuple):
        x_specs = [pl.BlockSpec((MIX_TM, k), lambda i: (jnp.minimum(i, N_PROMPT_TILES - 1), 0)),
                   pl.BlockSpec((MIX_TM, k), lambda i: (jnp.clip(i - N_PROMPT_TILES, 0, N_SAMPLE_TILES - 1), 0))]
    else:
        x_specs = [pl.BlockSpec((MIX_TM, k), lambda i: (jnp.minimum(i, last), 0))]
        x = (x,)
    state_spec = pl.BlockSpec((MIX_SUB,) + state.shape[1:], lambda i: (_sample_tile(i), 0, 0))
    return pl.pallas_call(
        functools.partial(_mix_block_body, n_x=len(x), n_params=len(params), n_outs=len(out_shapes),
                          layer=layer, prepare=prepare, mixer=mixer),
        grid=(N_MIX_TILES + 1,),
        in_specs=x_specs + [state_spec] + param_specs + [pl.BlockSpec(memory_space=pl.ANY)],
        out_specs=out_specs,
        out_shape=out_shapes,
        scratch_shapes=[pltpu.VMEM((k, n), BF16),
                        pltpu.VMEM((2, W_STAGE_ROWS, n), F32),
                        pltpu.VMEM((MIX_TM, n), F32),
                        pltpu.VMEM((MIX_TM, n), F32),
                        pltpu.SemaphoreType.DMA((2,))] + scratch_shapes,
        compiler_params=_cparams(1),
        name=name,
    )(*x, state, *params, w)


def _mix_out_specs(state_shape):
    return [pl.BlockSpec((MIX_TM, D), lambda i: (_mixed_tile(i), 0)),
            pl.BlockSpec((None,) + state_shape, lambda i: (0, 0, 0)),
            pl.BlockSpec((MIX_SUB,) + state_shape, lambda i: (_sample_tile(i), 0, 0))]


def _const_spec(shape):
    return pl.BlockSpec(shape, lambda i: (0,) * len(shape))


def _mix0_prepare(params):
    ws_ref = params[3]
    row = lax.broadcasted_iota(jnp.int32, (2 * STEP, 2 * STEP), 0)
    col = lax.broadcasted_iota(jnp.int32, (2 * STEP, 2 * STEP), 1)
    mask = (col // STEP) <= (row // STEP)
    return [jnp.where(mask, ws_ref[h], 0.0).astype(BF16) for h in range(SGU_HEADS)]


def _mix0_sub(p_ref, j, is_sample, st_ref, params, outs, scratch, w_masked):
    wc_ref, g_ref, b_ref, _, bs_ref = params
    y_ref, c3p_ref, c3s_ref, vrow_ref = outs
    ctx_ref, vprev_ref = scratch
    rows = slice(j * STEP, (j + 1) * STEP)

    z = p_ref[rows, 2 * D_A:3 * D_A] * p_ref[rows, 0:D_A]
    ctx_ref[6:8, :] = jnp.where(is_sample, _conv_operand(st_ref[j]), ctx_ref[6:8, :])
    zc = _conv_operand(z)
    ctx_ref[8:8 + STEP, :] = zc
    conv = (wc_ref[0:1, :] * ctx_ref[6:6 + STEP, :]
            + wc_ref[1:2, :] * ctx_ref[7:7 + STEP, :]
            + wc_ref[2:3, :] * zc)
    y_ref[rows, 0:D_A] = (p_ref[rows, D_A:2 * D_A] * conv).astype(y_ref.dtype)
    tail = z[STEP - 2:STEP, :]
    c3s_ref[j] = tail
    if j == MIX_SUB - 1:
        c3p_ref[...] = jnp.where(is_sample, c3p_ref[...], tail)
    ctx_ref[0:8, :] = ctx_ref[STEP:STEP + 8, :]

    u = jax.nn.gelu(p_ref[rows, 3 * D_A:3 * D_A + D_B])
    v = _ln(jax.nn.gelu(p_ref[rows, 3 * D_A + D_B:3 * D_A + 2 * D_B]), g_ref[...], b_ref[...])
    vrow_ref[j] = v
    vb = v.astype(BF16)
    for h in range(SGU_HEADS):
        hs = slice(h * 128, (h + 1) * 128)
        wm = w_masked[h]
        v_h = vb[:, hs]
        f = jnp.dot(wm[0:STEP, 0:STEP], v_h, preferred_element_type=F32) + bs_ref[h, 0:STEP, :]
        if j % 2 == 1:
            f_second = (jnp.dot(wm[STEP:, 0:STEP], vprev_ref[:, hs], preferred_element_type=F32)
                        + jnp.dot(wm[STEP:, STEP:], v_h, preferred_element_type=F32)
                        + bs_ref[h, STEP:, :])
            f = jnp.where(is_sample, f, f_second)
        y_ref[rows, D_A + h * 128:D_A + (h + 1) * 128] = (u[:, hs] * f).astype(y_ref.dtype)
    vprev_ref[...] = vb


def _mix0(x, w_in, layer, state_c3, w_conv3, g, b, w_spatial, b_spatial_col):
    params = [w_conv3, g, b, w_spatial, b_spatial_col]
    param_specs = [_const_spec((3, D_A)), _const_spec((1, D_B)), _const_spec((1, D_B)),
                   _const_spec((SGU_HEADS, 128, 128)), _const_spec((SGU_HEADS, 128, 1))]
    out_shapes = [jax.ShapeDtypeStruct((M_TOK, D), BF16),
                  jax.ShapeDtypeStruct((1, 2, D_A), F32),
                  jax.ShapeDtypeStruct((N_STREAMS, 2, D_A), F32),
                  jax.ShapeDtypeStruct((N_STREAMS, STEP, D_B), F32)]
    out_specs = _mix_out_specs((2, D_A)) + [
        pl.BlockSpec((MIX_SUB, STEP, D_B), lambda i: (_sample_tile(i), 0, 0))]
    scratch = [pltpu.VMEM((STEP + 8, D_A), F32), pltpu.VMEM((STEP, D_B), BF16)]
    return _mix_block(x, w_in, layer, state_c3, params, param_specs, out_shapes, out_specs, scratch,
                      _mix0_prepare, _mix0_sub, "mix_even")


CONF_W = 31
CTX_PAD = 32
CONV_COLS = 256
SUBLANES = 8
SHIFT_ROWS = CTX_PAD + STEP - SUBLANES


def _mix1_sub(p_ref, j, is_sample, st_ref, params, outs, scratch, _):
    w_ref, bdw_ref, g_ref, b_ref = params
    y_ref, c31p_ref, c31s_ref = outs
    ctx_ref, cv_ref, sh_ref = scratch
    rows = slice(j * STEP, (j + 1) * STEP)
    first = CTX_PAD - (CONF_W - 1)

    glu = p_ref[rows, 0:D] * jax.nn.sigmoid(p_ref[rows, D:2 * D])
    ctx_ref[first:CTX_PAD, :] = jnp.where(is_sample, _conv_operand(st_ref[j]), ctx_ref[first:CTX_PAD, :])
    ctx_ref[CTX_PAD:CTX_PAD + STEP, :] = _conv_operand(glu)
    tail = glu[STEP - (CONF_W - 1):STEP, :]
    c31s_ref[j] = tail
    if j == MIX_SUB - 1:
        c31p_ref[...] = jnp.where(is_sample, c31p_ref[...], tail)
    for c in range(D // CONV_COLS):
        cs = slice(c * CONV_COLS, (c + 1) * CONV_COLS)
        for s in range(1, SUBLANES):
            sh_ref[s - 1, :, :] = ctx_ref[s:s + SHIFT_ROWS, cs]
        acc = None
        for k in range(CONF_W):
            q, s = divmod(first + k, SUBLANES)
            taps = slice(q * SUBLANES, q * SUBLANES + STEP)
            operand = ctx_ref[taps, cs] if s == 0 else sh_ref[s - 1, taps, :]
            term = w_ref[k:k + 1, cs] * operand
            acc = term if acc is None else acc + term
        cv_ref[:, cs] = acc + bdw_ref[:, cs]
    c = _ln(cv_ref[...], g_ref[...], b_ref[...])
    y_ref[rows, :] = (c * jax.nn.sigmoid(c)).astype(y_ref.dtype)
    ctx_ref[0:CTX_PAD, :] = ctx_ref[STEP:STEP + CTX_PAD, :]


def _mix1(x, w_pw1, layer, state_c31, w_dw, b_dw, g, b):
    params = [w_dw, b_dw, g, b]
    param_specs = [_const_spec((CONF_W, D)), _const_spec((1, D)), _const_spec((1, D)), _const_spec((1, D))]
    out_shapes = [jax.ShapeDtypeStruct((M_TOK, D), BF16),
                  jax.ShapeDtypeStruct((1, CONF_W - 1, D), F32),
                  jax.ShapeDtypeStruct((N_STREAMS, CONF_W - 1, D), F32)]
    scratch = [pltpu.VMEM((CTX_PAD + STEP, D), F32),
               pltpu.VMEM((STEP, D), F32),
               pltpu.VMEM((SUBLANES - 1, SHIFT_ROWS, CONV_COLS), F32)]
    return _mix_block(x, w_pw1, layer, state_c31, params, param_specs, out_shapes,
                      _mix_out_specs((CONF_W - 1, D)), scratch, lambda params: None, _mix1_sub, "mix_odd")


ATTN_TM_PROMPT = 512
ATTN_TM_SAMPLE = 128


def _attn_block_body(x_ref, k_ref, v_ref, g_ref, b_ref, wq_hbm, wo_hbm, *rest,
                     layer, n_tiles, streams):
    o_ref, wq_b, wo_b, stage, acc0, acc1, x_prev, q_scr, a_scr, sem = rest[-10:]
    i = pl.program_id(0)
    rows_per_stream = x_ref.shape[0] // streams

    @pl.when(i == 0)
    def _():
        acc1[...] = jnp.zeros_like(acc1)
        x_prev[...] = jnp.zeros_like(x_prev)
        _load_weight_bf16(wq_hbm.at[layer], wq_b, stage, sem)
        _load_weight_bf16(wo_hbm.at[layer], wo_b, stage, sem)

    def step(acc_new, acc_old):
        x = x_ref[...]
        q_scr[...] = jnp.dot(x.astype(BF16), wq_b[...], preferred_element_type=F32).astype(BF16)
        for s in range(streams):
            rows = slice(s * rows_per_stream, (s + 1) * rows_per_stream)
            for h in range(HEADS):
                hs = slice(h * HEAD_DIM, (h + 1) * HEAD_DIM)
                sc = lax.dot_general(q_scr[rows, hs], k_ref[s, :, hs].astype(BF16),
                                     (((1,), (1,)), ((), ())), preferred_element_type=F32) * ATTN_SCALE
                p = jnp.exp(sc - jnp.max(sc, axis=-1, keepdims=True))
                p = p / jnp.sum(p, axis=-1, keepdims=True)
                a_scr[rows, hs] = _bdot(p, v_ref[s, :, hs]).astype(BF16)
        acc_new[...] = jnp.dot(a_scr[...], wo_b[...], preferred_element_type=F32)
        o_ref[...] = _ln(ALPHA * x_prev[...] + acc_old[...], g_ref[...], b_ref[...])
        x_prev[...] = x

    @pl.when(jnp.logical_and(i <= n_tiles, i % 2 == 0))
    def _():
        step(acc0, acc1)

    @pl.when(jnp.logical_and(i <= n_tiles, i % 2 == 1))
    def _():
        step(acc1, acc0)

    @pl.when(i > n_tiles)
    def _():
        o_ref[...] = jnp.zeros_like(o_ref)


def _attn_block_call(x, k, v, g, b, w_q, w_o, layer, *, tm, streams, first_tile, n_tiles, n_clear,
                     prev_out, name):
    last = n_tiles - 1

    def in_tile(i):
        return (first_tile + jnp.minimum(i, last), 0)

    def kv_block(i):
        return (layer, jnp.minimum(i, last) if k.shape[1] > streams else 0, 0, 0)

    def out_tile(i):
        return (first_tile + jnp.maximum(i - 1, 0), 0)

    in_specs = [pl.BlockSpec((tm, D), in_tile),
                pl.BlockSpec((None, streams, N_MEM, D), kv_block),
                pl.BlockSpec((None, streams, N_MEM, D), kv_block),
                pl.BlockSpec((1, D), lambda i: (0, 0)),
                pl.BlockSpec((1, D), lambda i: (0, 0)),
                pl.BlockSpec(memory_space=pl.ANY),
                pl.BlockSpec(memory_space=pl.ANY)]
    args = [x, k, v, g, b, w_q, w_o]
    aliases = {}
    if prev_out is not None:
        in_specs.append(pl.BlockSpec(memory_space=pl.ANY))
        args.append(prev_out)
        aliases = {len(args) - 1: 0}
    return pl.pallas_call(
        functools.partial(_attn_block_body, layer=layer, n_tiles=n_tiles, streams=streams),
        grid=(n_tiles + 1 + n_clear,),
        in_specs=in_specs,
        out_specs=pl.BlockSpec((tm, D), out_tile),
        out_shape=jax.ShapeDtypeStruct((M_TOK, D), F32),
        scratch_shapes=[pltpu.VMEM((D, D), BF16),
                        pltpu.VMEM((D, D), BF16),
                        pltpu.VMEM((2, W_STAGE_ROWS, D), F32),
                        pltpu.VMEM((tm, D), F32),
                        pltpu.VMEM((tm, D), F32),
                        pltpu.VMEM((tm, D), F32),
                        pltpu.VMEM((tm, D), BF16),
                        pltpu.VMEM((tm, D), BF16),
                        pltpu.SemaphoreType.DMA((2,))],
        input_output_aliases=aliases,
        compiler_params=_cparams(1),
        name=name,
    )(*args)


def _attn_block(x, pk, pv, ck, cv, g, b, w_q, w_o, layer):
    n_sample = N_STREAMS * STEP
    out = _attn_block_call(x, pk, pv, g, b, w_q, w_o, layer, tm=ATTN_TM_PROMPT, streams=1,
                           first_tile=0, n_tiles=N_PROMPT // ATTN_TM_PROMPT,
                           n_clear=n_sample // ATTN_TM_PROMPT, prev_out=None, name="attn_prompt")
    return _attn_block_call(x, ck, cv, g, b, w_q, w_o, layer, tm=ATTN_TM_SAMPLE,
                            streams=ATTN_TM_SAMPLE // STEP, first_tile=N_PROMPT // ATTN_TM_SAMPLE,
                            n_tiles=n_sample // ATTN_TM_SAMPLE, n_clear=0, prev_out=out,
                            name="attn_sample")


TOP_K = 2
ROUTE_TM = 512
EXP_TM = 512
N_ASSIGN = M_TOK * TOP_K
N_EXP_TILES = N_ASSIGN // EXP_TM
N_VISITS = N_EXP_TILES + N_EXPERTS - 1
MOVE_TM = 256
LANE = 128
CHUNKS = D // LANE
ISSUE_UNROLL = 8


def _to_chunk_major(dst_ref, x, rows):
    for c in range(CHUNKS):
        dst_ref[pl.ds(c, rows, stride=CHUNKS), :] = x[:, c * LANE:(c + 1) * LANE]


def _from_chunk_major(src_ref, rows):
    return jnp.concatenate([src_ref[pl.ds(c, rows, stride=CHUNKS), :] for c in range(CHUNKS)], axis=-1)


def _route_body(x_ref, wr_ref, br_ref, idx_ref, wts_ref, cnt_ref, run_ref):
    i = pl.program_id(0)

    @pl.when(i == 0)
    def _():
        run_ref[...] = jnp.zeros_like(run_ref)

    logits = _bdot(x_ref[...], wr_ref[...]) + br_ref[...]
    lane = lax.broadcasted_iota(jnp.int32, logits.shape, 1)

    def first_argmax(vals):
        top = jnp.max(vals, axis=-1, keepdims=True)
        idx = jnp.min(jnp.where(vals == top, lane, ROUTE_LANES), axis=-1, keepdims=True)
        return top, idx

    g_logits = jnp.where(lane < N_GROUPS, logits, NEG)
    g_top, g_idx = first_argmax(g_logits)
    p_group = 1.0 / jnp.sum(jnp.exp(g_logits - g_top), axis=-1, keepdims=True)
    lo_lane = N_GROUPS + E_PER * g_idx
    e_logits = jnp.where(jnp.logical_and(lane >= lo_lane, lane < lo_lane + E_PER), logits, NEG)
    v1, i1 = first_argmax(e_logits)
    v2, i2 = first_argmax(jnp.where(lane == i1, NEG, e_logits))
    t = jnp.exp(v2 - v1)
    w1 = p_group / (1.0 + t)
    w2 = p_group * t / (1.0 + t)

    a1 = (lane == i1).astype(F32)
    a2 = (lane == i2).astype(F32)
    a = a1 + a2
    tm = logits.shape[0]
    earlier = (lax.broadcasted_iota(jnp.int32, (tm, tm), 1)
               < lax.broadcasted_iota(jnp.int32, (tm, tm), 0)).astype(BF16)
    before = jnp.dot(earlier, a.astype(BF16), preferred_element_type=F32) + run_ref[...]
    r1 = jnp.sum(a1 * before, axis=-1, keepdims=True)
    r2 = jnp.sum(a2 * before, axis=-1, keepdims=True)
    run_ref[...] += jnp.sum(a, axis=0, keepdims=True)
    cnt_ref[...] = run_ref[...].astype(jnp.int32)

    e1 = (i1 - N_GROUPS).astype(F32)
    e2 = (i2 - N_GROUPS).astype(F32)
    packed = (jnp.where(lane == 0, e1, 0.0) + jnp.where(lane == 1, e2, 0.0)
              + jnp.where(lane == 2, r1, 0.0) + jnp.where(lane == 3, r2, 0.0))
    idx_ref[...] = packed.T[0:8, :].astype(jnp.int32)
    wts_ref[...] = jnp.where(lane == 0, w1, 0.0) + jnp.where(lane == 1, w2, 0.0)


def _route(x, wr, br):
    m = x.shape[0]
    tm = ROUTE_TM
    return pl.pallas_call(
        _route_body,
        grid=(m // tm,),
        in_specs=[pl.BlockSpec((tm, D), lambda i: (i, 0)),
                  pl.BlockSpec((D, ROUTE_LANES), lambda i: (0, 0)),
                  pl.BlockSpec((1, ROUTE_LANES), lambda i: (0, 0))],
        out_specs=[pl.BlockSpec((8, tm), lambda i: (0, i)),
                   pl.BlockSpec((tm, ROUTE_LANES), lambda i: (i, 0)),
                   pl.BlockSpec((1, ROUTE_LANES), lambda i: (0, 0))],
        out_shape=[jax.ShapeDtypeStruct((8, m), jnp.int32),
                   jax.ShapeDtypeStruct((m, ROUTE_LANES), F32),
                   jax.ShapeDtypeStruct((1, ROUTE_LANES), jnp.int32)],
        scratch_shapes=[pltpu.VMEM((1, ROUTE_LANES), F32)],
        compiler_params=_cparams(1),
        name="moe_route",
    )(x, wr, br)


def _plan_body(cnt_ref, idx_ref, pos_ref, vt_ref, ve_ref, vlo_ref, vhi_ref, nvis_ref):
    shift = EXP_TM.bit_length() - 1
    e1 = idx_ref[0:1, :]
    e2 = idx_ref[1:2, :]
    pos1 = idx_ref[2:3, :]
    pos2 = idx_ref[3:4, :]
    start = jnp.int32(0)
    n_vis = jnp.int32(0)
    last_e = jnp.int32(0)
    for e in range(N_EXPERTS):
        cnt = cnt_ref[N_GROUPS + e]
        end = start + cnt
        pos1 = pos1 + jnp.where(e1 == e, start, 0)
        pos2 = pos2 + jnp.where(e2 == e, start, 0)
        first_tile = lax.shift_right_logical(start, shift)
        n_tiles = jnp.where(cnt > 0, lax.shift_right_logical(end + (EXP_TM - 1), shift) - first_tile, 0)

        def add_visit(k, carry, e=e, start=start, end=end, first_tile=first_tile, n_vis=n_vis):
            tile = first_tile + k
            vt_ref[n_vis + k] = tile
            ve_ref[n_vis + k] = e
            vlo_ref[n_vis + k] = jnp.maximum(start - tile * EXP_TM, 0)
            vhi_ref[n_vis + k] = jnp.minimum(end - tile * EXP_TM, EXP_TM)
            return carry

        lax.fori_loop(0, n_tiles, add_visit, 0)
        n_vis = n_vis + n_tiles
        last_e = jnp.where(cnt > 0, e, last_e)
        start = end
    nvis_ref[0] = n_vis

    def add_idle(k, carry):
        vt_ref[k] = N_EXP_TILES - 1
        ve_ref[k] = last_e
        vlo_ref[k] = 0
        vhi_ref[k] = 0
        return carry

    lax.fori_loop(n_vis, N_VISITS, add_idle, 0)
    pos_ref[...] = jnp.concatenate([pos1, pos2, jnp.zeros((6, M_TOK), jnp.int32)], axis=0)


def _plan(cnt, idx):
    smem = pl.BlockSpec(memory_space=pltpu.SMEM)
    vmem = pl.BlockSpec(memory_space=pltpu.VMEM)
    table = jax.ShapeDtypeStruct((N_VISITS,), jnp.int32)
    return pl.pallas_call(
        _plan_body,
        in_specs=[smem, vmem],
        out_specs=[vmem, smem, smem, smem, smem, smem],
        out_shape=[jax.ShapeDtypeStruct((8, M_TOK), jnp.int32), table, table, table, table,
                   jax.ShapeDtypeStruct((1,), jnp.int32)],
        name="moe_plan",
    )(cnt, idx)


def _row_copy(src_ref, src_row, dst_ref, dst_row, sem):
    return pltpu.make_async_copy(src_ref.at[pl.ds(src_row * CHUNKS, CHUNKS)],
                                 dst_ref.at[pl.ds(dst_row * CHUNKS, CHUNKS)], sem)


def _scatter_body(pos1_ref, pos2_ref, x_ref, xs_hbm, stage, sem):
    i = pl.program_id(0)
    last = pl.num_programs(0) - 1
    slot = i % 2

    def wait_slot(s):
        for _ in range(TOP_K):
            pltpu.make_async_copy(stage.at[s], xs_hbm.at[pl.ds(0, MOVE_TM * CHUNKS)], sem.at[s]).wait()

    @pl.when(i >= 2)
    def _():
        wait_slot(slot)

    _to_chunk_major(stage.at[slot], x_ref[...], MOVE_TM)

    def issue(r, carry):
        t = i * MOVE_TM + r
        _row_copy(stage.at[slot], r, xs_hbm, pos1_ref[t], sem.at[slot]).start()
        _row_copy(stage.at[slot], r, xs_hbm, pos2_ref[t], sem.at[slot]).start()
        return carry

    lax.fori_loop(0, MOVE_TM, issue, 0, unroll=ISSUE_UNROLL)

    @pl.when(i == last)
    def _():
        wait_slot(1 - slot)
        wait_slot(slot)


def _scatter(pos1, pos2, x):
    m = x.shape[0]
    assert (m // MOVE_TM) >= 2
    grid_spec = pltpu.PrefetchScalarGridSpec(
        num_scalar_prefetch=2,
        grid=(m // MOVE_TM,),
        in_specs=[pl.BlockSpec((MOVE_TM, D), lambda i, p1, p2: (i, 0))],
        out_specs=pl.BlockSpec(memory_space=pl.ANY),
        scratch_shapes=[pltpu.VMEM((2, MOVE_TM * CHUNKS, LANE), F32),
                        pltpu.SemaphoreType.DMA((2,))])
    return pl.pallas_call(
        _scatter_body,
        grid_spec=grid_spec,
        out_shape=jax.ShapeDtypeStruct((N_ASSIGN * CHUNKS, LANE), F32),
        compiler_params=_cparams(1),
        name="moe_scatter",
    )(pos1, pos2, x)


def _experts_body(vt_ref, ve_ref, vlo_ref, vhi_ref, nvis_ref, x_ref, wg_ref, wu_ref, wd_ref,
                  o_ref, wg_b, wu_b, wd_b):
    v = pl.program_id(0)
    prev = jnp.maximum(v - 1, 0)

    @pl.when(v < nvis_ref[0])
    def _():
        @pl.when(jnp.logical_or(v == 0, ve_ref[v] != ve_ref[prev]))
        def _():
            wg_b[...] = wg_ref[...].astype(BF16)
            wu_b[...] = wu_ref[...].astype(BF16)
            wd_b[...] = wd_ref[...].astype(BF16)

        xb = _from_chunk_major(x_ref, EXP_TM).astype(BF16)
        gate = jnp.dot(xb, wg_b[...], preferred_element_type=F32)
        up = jnp.dot(xb, wu_b[...], preferred_element_type=F32)
        h = (gate * jax.nn.sigmoid(gate) * up).astype(BF16)
        y = jnp.dot(h, wd_b[...], preferred_element_type=F32)
        row = lax.broadcasted_iota(jnp.int32, (EXP_TM, 1), 0)
        mine = jnp.logical_and(row >= vlo_ref[v], row < vhi_ref[v])
        first_visit = jnp.logical_or(v == 0, vt_ref[v] != vt_ref[prev])

        @pl.when(first_visit)
        def _():
            _to_chunk_major(o_ref, jnp.where(mine, y, 0.0), EXP_TM)

        @pl.when(jnp.logical_not(first_visit))
        def _():
            _to_chunk_major(o_ref, jnp.where(mine, y, _from_chunk_major(o_ref, EXP_TM)), EXP_TM)


def _experts(vt, ve, vlo, vhi, nvis, xs, w_gate, w_up, w_down, layer):
    def tile_map(v, vt, ve, vlo, vhi, nvis):
        return (vt[v], 0)

    def weight_map(v, vt, ve, vlo, vhi, nvis):
        return (layer, ve[v], 0, 0)

    grid_spec = pltpu.PrefetchScalarGridSpec(
        num_scalar_prefetch=5,
        grid=(N_VISITS,),
        in_specs=[pl.BlockSpec((EXP_TM * CHUNKS, LANE), tile_map),
                  pl.BlockSpec((None, None, D, D_EXPERT), weight_map),
                  pl.BlockSpec((None, None, D, D_EXPERT), weight_map),
                  pl.BlockSpec((None, None, D_EXPERT, D), weight_map)],
        out_specs=pl.BlockSpec((EXP_TM * CHUNKS, LANE), tile_map),
        scratch_shapes=[pltpu.VMEM((D, D_EXPERT), BF16),
                        pltpu.VMEM((D, D_EXPERT), BF16),
                        pltpu.VMEM((D_EXPERT, D), BF16)])
    return pl.pallas_call(
        _experts_body,
        grid_spec=grid_spec,
        out_shape=jax.ShapeDtypeStruct((N_ASSIGN * CHUNKS, LANE), F32),
        compiler_params=_cparams(1),
        name="moe_experts",
    )(vt, ve, vlo, vhi, nvis, xs, w_gate, w_up, w_down)


def _combine_body(pos1_ref, pos2_ref, x_ref, wts_ref, g_ref, b_ref, ys_hbm, *rest, split):
    out_refs, (ybuf, sem) = rest[:-2], rest[-2:]
    i = pl.program_id(0)
    slot = i % 2

    def gather_rows(step, s):
        def issue(r, carry):
            t = step * MOVE_TM + r
            _row_copy(ys_hbm, pos1_ref[t], ybuf.at[s, 0], r, sem.at[s]).start()
            _row_copy(ys_hbm, pos2_ref[t], ybuf.at[s, 1], r, sem.at[s]).start()
            return carry
        lax.fori_loop(0, MOVE_TM, issue, 0, unroll=ISSUE_UNROLL)

    @pl.when(i == 0)
    def _():
        gather_rows(0, 0)

    @pl.when(i + 1 < pl.num_programs(0))
    def _():
        gather_rows(i + 1, 1 - slot)

    for k in range(TOP_K):
        pltpu.make_async_copy(ys_hbm.at[pl.ds(0, MOVE_TM * CHUNKS)], ybuf.at[slot, k], sem.at[slot]).wait()
    y = (wts_ref[:, 0:1] * _from_chunk_major(ybuf.at[slot, 0], MOVE_TM)
         + wts_ref[:, 1:2] * _from_chunk_major(ybuf.at[slot, 1], MOVE_TM))
    out = _ln(ALPHA * x_ref[...] + y, g_ref[...], b_ref[...])
    if split is None:
        out_refs[0][...] = out
    else:
        @pl.when(i < split)
        def _():
            out_refs[0][...] = out

        @pl.when(i >= split)
        def _():
            out_refs[1][...] = out


def _combine(pos1, pos2, x, wts, g, b, ys, split_rows=None):
    m = x.shape[0]
    if split_rows is None:
        split = None
        out_specs = pl.BlockSpec((MOVE_TM, D), lambda i, p1, p2: (i, 0))
        out_shape = jax.ShapeDtypeStruct((m, D), F32)
    else:
        split = split_rows // MOVE_TM
        out_specs = [pl.BlockSpec((MOVE_TM, D), lambda i, p1, p2: (jnp.minimum(i, split - 1), 0)),
                     pl.BlockSpec((MOVE_TM, D), lambda i, p1, p2: (jnp.maximum(i - split, 0), 0))]
        out_shape = [jax.ShapeDtypeStruct((split_rows, D), F32),
                     jax.ShapeDtypeStruct((m - split_rows, D), F32)]
    grid_spec = pltpu.PrefetchScalarGridSpec(
        num_scalar_prefetch=2,
        grid=(m // MOVE_TM,),
        in_specs=[pl.BlockSpec((MOVE_TM, D), lambda i, p1, p2: (i, 0)),
                  pl.BlockSpec((MOVE_TM, ROUTE_LANES), lambda i, p1, p2: (i, 0)),
                  pl.BlockSpec((1, D), lambda i, p1, p2: (0, 0)),
                  pl.BlockSpec((1, D), lambda i, p1, p2: (0, 0)),
                  pl.BlockSpec(memory_space=pl.ANY)],
        out_specs=out_specs,
        scratch_shapes=[pltpu.VMEM((2, TOP_K, MOVE_TM * CHUNKS, LANE), F32),
                        pltpu.SemaphoreType.DMA((2,))])
    return pl.pallas_call(
        functools.partial(_combine_body, split=split),
        grid_spec=grid_spec,
        out_shape=out_shape,
        compiler_params=_cparams(1),
        name="moe_combine",
    )(pos1, pos2, x, wts, g, b, ys)


def _moe(x, wr, br, w_gate, w_up, w_down, layer, g, b, split_rows=None):
    idx, wts, cnt = _route(x, wr, br)
    pos, vt, ve, vlo, vhi, nvis = _plan(cnt.reshape(ROUTE_LANES), idx)
    xs = _scatter(pos[0], pos[1], x)
    ys = _experts(vt, ve, vlo, vhi, nvis, xs, w_gate, w_up, w_down, layer)
    return _combine(pos[0], pos[1], x, wts, g, b, ys, split_rows)


def _router_params(w_rg, b_rg, w_re, b_re):
    w = jnp.concatenate([w_rg, jnp.transpose(w_re, (1, 0, 2)).reshape(D, N_EXPERTS)], axis=1)
    bias = jnp.concatenate([b_rg, b_re.reshape(N_EXPERTS)])
    pad = ROUTE_LANES - w.shape[1]
    return jnp.pad(w, ((0, 0), (0, pad))), jnp.pad(bias, (0, pad))[None, :]


def kernel(x_prompt, x_sample, mem_prompt, cache_mem_k, cache_mem_v, state_conv3, state_conv31,
           ln_gain, ln_bias, w_in_ab, w_conv3, ln_sgu_gain, ln_sgu_bias, w_spatial, b_spatial, w_out_ab,
           w_pw1, w_dw31, b_dw31, ln_conf_gain, ln_conf_bias, w_pw2,
           w_mem_q, w_mem_k, w_mem_v, w_mem_o,
           w_route_group, b_route_group, w_route_expert, b_route_expert,
           w_exp_gate, w_exp_up, w_exp_down):
    depth = ln_gain.shape[0]
    x = (x_prompt.reshape(N_PROMPT, D), x_sample.reshape(N_STREAMS * STEP, D))

    mem = mem_prompt.reshape(N_MEM, D)
    pk, pk_b = _mem_kv(mem, w_mem_k, "mem_k")
    pv, pv_b = _mem_kv(mem, w_mem_v, "mem_v")
    pk_b = pk_b.reshape(depth, 1, N_MEM, D)
    pv_b = pv_b.reshape(depth, 1, N_MEM, D)
    ck = cache_mem_k.reshape(depth, N_STREAMS, N_MEM, D)
    cv = cache_mem_v.reshape(depth, N_STREAMS, N_MEM, D)

    def ln_params(l, k):
        return ln_gain[l, k][None, :], ln_bias[l, k][None, :]

    c3_prompt, c3_sample, c31_prompt, c31_sample, v_rows = [], [], [], [], []
    for l in range(depth):
        i = l // 2
        if l % 2 == 0:
            y_mix, c3p, c3s, v = _mix0(
                x, w_in_ab, i, state_conv3[i], w_conv3[i], ln_sgu_gain[i][None, :], ln_sgu_bias[i][None, :],
                w_spatial[i], b_spatial[i][:, :, None])
            c3_prompt.append(c3p)
            c3_sample.append(c3s)
            v_rows.append(v)
            w_out = w_out_ab
        else:
            y_mix, c31p, c31s = _mix1(
                x, w_pw1, i, state_conv31[i], w_dw31[i], b_dw31[i][None, :],
                ln_conf_gain[i][None, :], ln_conf_bias[i][None, :])
            c31_prompt.append(c31p)
            c31_sample.append(c31s)
            w_out = w_pw2
        x = _mm_ln(y_mix, w_out, i, x, *ln_params(l, 0), name="mix_out_ln")
        x = _attn_block(x, pk_b, pv_b, ck, cv, *ln_params(l, 1), w_mem_q, w_mem_o, l)
        wr, br = _router_params(w_route_group[l], b_route_group[l], w_route_expert[l], b_route_expert[l])
        x = _moe(x, wr, br, w_exp_gate, w_exp_up, w_exp_down, l, *ln_params(l, 2),
                 split_rows=N_PROMPT if l == depth - 1 else None)

    y_prompt = x[0].reshape(1, N_PROMPT, D)
    y_sample = x[1].reshape(N_STREAMS, STEP, D)
    prompt_mem_k = pk.reshape(depth, 1, N_MEM, HEADS, HEAD_DIM)
    prompt_mem_v = pv.reshape(depth, 1, N_MEM, HEADS, HEAD_DIM)
    prompt_conv3 = jnp.stack(c3_prompt)
    sample_conv3 = jnp.stack(c3_sample)
    prompt_conv31 = jnp.stack(c31_prompt)
    sample_conv31 = jnp.stack(c31_sample)
    sample_sgu_v = jnp.stack(v_rows)
    return (y_prompt, y_sample, prompt_mem_k, prompt_mem_v, prompt_conv3, prompt_conv31,
            sample_conv3, sample_conv31, sample_sgu_v)
```

```python
import functools

import jax
import jax.numpy as jnp
from jax import lax
from jax.experimental import pallas as pl
from jax.experimental.pallas import tpu as pltpu

F32 = jnp.float32
BF16 = jnp.bfloat16

D = 2048
N_PROMPT = 8192
N_STREAMS = 16
STEP = 64
NP_STEPS = N_PROMPT // STEP
M_TOK = N_PROMPT + N_STREAMS * STEP
N_MEM = 256
HEADS = 4
HEAD_DIM = D // HEADS
D_A = 1024
D_B = 1024
SGU_HEADS = 8
N_GROUPS = 4
E_PER = 4
N_EXPERTS = 16
D_EXPERT = 512
LN_EPS = 1e-5
ALPHA = (2.0 * 2) ** 0.25
ATTN_SCALE = HEAD_DIM ** -0.5
ROUTE_LANES = 128
NEG = -1e30

VMEM_LIMIT_BYTES = 60000 * 1024


def _cparams(n_axes):
    return pltpu.CompilerParams(
        dimension_semantics=("arbitrary",) * n_axes,
        vmem_limit_bytes=VMEM_LIMIT_BYTES)


def _ln(x, g, b):
    mu = jnp.mean(x, axis=-1, keepdims=True)
    xc = x - mu
    var = jnp.mean(xc * xc, axis=-1, keepdims=True)
    return xc * lax.rsqrt(var + LN_EPS) * g + b


def _bdot(a, b):
    return jnp.dot(a.astype(BF16), b.astype(BF16), preferred_element_type=F32)


def _conv_operand(x):
    return x.astype(BF16).astype(F32)


def _mem_kv_body(x_ref, w_ref, o_ref, ob_ref):
    y = _bdot(x_ref[...], w_ref[...])
    o_ref[...] = y
    ob_ref[...] = y.astype(BF16)


def _mem_kv(mem, w, name):
    depth, k, n = w.shape
    tn = 512
    out_block = pl.BlockSpec((None, N_MEM, tn), lambda l, j: (l, 0, j))
    return pl.pallas_call(
        _mem_kv_body,
        grid=(depth, n // tn),
        in_specs=[pl.BlockSpec((N_MEM, k), lambda l, j: (0, 0)),
                  pl.BlockSpec((None, k, tn), lambda l, j: (l, 0, j))],
        out_specs=[out_block, out_block],
        out_shape=[jax.ShapeDtypeStruct((depth, N_MEM, n), F32),
                   jax.ShapeDtypeStruct((depth, N_MEM, n), BF16)],
        compiler_params=_cparams(2),
        name=name,
    )(mem, w)


MM_LN_TM = 512
W_STAGE_ROWS = 128


def _load_weight_bf16(w_hbm, wb, stage, sem):
    rows = stage.shape[1]
    n_chunks = wb.shape[0] // rows

    def copy(k):
        return pltpu.make_async_copy(w_hbm.at[pl.ds(k * rows, rows)], stage.at[k % 2], sem.at[k % 2])

    copy(0).start()
    for k in range(n_chunks):
        if k + 1 < n_chunks:
            copy(k + 1).start()
        copy(k).wait()
        wb[k * rows:(k + 1) * rows, :] = stage[k % 2].astype(BF16)


def _two_stage_steps(i, n_tiles, buf0, buf1, fill, finish):
    @pl.when(jnp.logical_and(i < n_tiles, i % 2 == 0))
    def _():
        fill(buf0)
        finish(buf1)

    @pl.when(jnp.logical_and(i < n_tiles, i % 2 == 1))
    def _():
        fill(buf1)
        finish(buf0)

    @pl.when(i == n_tiles)
    def _():
        finish(buf1 if n_tiles % 2 == 0 else buf0)


def _mm_ln_body(x_ref, *rest, layer, res_split, n_tiles):
    if res_split is None:
        r_ref, g_ref, b_ref, w_hbm, o_ref, wb, stage, acc0, acc1, sem = rest
    else:
        ra_ref, rb_ref, g_ref, b_ref, w_hbm, o_ref, wb, stage, acc0, acc1, sem = rest
    i = pl.program_id(0)

    @pl.when(i == 0)
    def _():
        acc1[...] = jnp.zeros_like(acc1)
        _load_weight_bf16(w_hbm.at[layer], wb, stage, sem)

    def fill(acc):
        acc[...] = jnp.dot(x_ref[...], wb[...], preferred_element_type=F32)

    def finish(acc):
        if res_split is None:
            res = r_ref[...]
        else:
            res = jnp.where(i - 1 < res_split, ra_ref[...], rb_ref[...])
        o_ref[...] = _ln(ALPHA * res + acc[...], g_ref[...], b_ref[...])

    _two_stage_steps(i, n_tiles, acc0, acc1, fill, finish)


def _mm_ln(x, w, layer, res, g, b, name):
    m, k = x.shape
    n = w.shape[-1]
    tm = MM_LN_TM
    n_tiles = m // tm
    assert x.dtype == BF16

    def done_tile(i):
        return (jnp.maximum(i - 1, 0), 0)

    if isinstance(res, tuple):
        res_split = res[0].shape[0] // tm
        res_specs = [pl.BlockSpec((tm, n), lambda i: (jnp.clip(i - 1, 0, res_split - 1), 0)),
                     pl.BlockSpec((tm, n), lambda i: (jnp.maximum(i - 1 - res_split, 0), 0))]
    else:
        res_split = None
        res_specs = [pl.BlockSpec((tm, n), done_tile)]
        res = (res,)

    return pl.pallas_call(
        functools.partial(_mm_ln_body, layer=layer, res_split=res_split, n_tiles=n_tiles),
        grid=(n_tiles + 1,),
        in_specs=[pl.BlockSpec((tm, k), lambda i: (jnp.minimum(i, n_tiles - 1), 0))] + res_specs + [
                  pl.BlockSpec((1, n), lambda i: (0, 0)),
                  pl.BlockSpec((1, n), lambda i: (0, 0)),
                  pl.BlockSpec(memory_space=pl.ANY)],
        out_specs=pl.BlockSpec((tm, n), done_tile),
        out_shape=jax.ShapeDtypeStruct((m, n), F32),
        scratch_shapes=[pltpu.VMEM((k, n), BF16),
                        pltpu.VMEM((2, W_STAGE_ROWS, n), F32),
                        pltpu.VMEM((tm, n), F32),
                        pltpu.VMEM((tm, n), F32),
                        pltpu.SemaphoreType.DMA((2,))],
        compiler_params=_cparams(1),
        name=name,
    )(x, *res, g, b, w)


MIX_TM = 256
MIX_SUB = MIX_TM // STEP
N_PROMPT_TILES = N_PROMPT // MIX_TM
N_MIX_TILES = M_TOK // MIX_TM
N_SAMPLE_TILES = N_MIX_TILES - N_PROMPT_TILES


def _mix_block_body(*refs, n_x, n_params, n_outs, layer, prepare, mixer):
    x_refs, refs = refs[:n_x], refs[n_x:]
    st_ref, refs = refs[0], refs[1:]
    params, refs = refs[:n_params], refs[n_params:]
    w_hbm, refs = refs[0], refs[1:]
    outs, refs = refs[:n_outs], refs[n_outs:]
    wb, stage, p0, p1, sem = refs[:5]
    scratch = refs[5:]
    i = pl.program_id(0)

    @pl.when(i == 0)
    def _():
        p1[...] = jnp.zeros_like(p1)
        for ref in tuple(scratch) + tuple(outs[1:2]):
            ref[...] = jnp.zeros_like(ref)
        _load_weight_bf16(w_hbm.at[layer], wb, stage, sem)

    def fill(p_new):
        if n_x == 1:
            x = x_refs[0][...]
        else:
            x = jnp.where(i < N_PROMPT_TILES, x_refs[0][...], x_refs[1][...])
        p_new[...] = jnp.dot(x.astype(BF16), wb[...], preferred_element_type=F32)

    def finish(p_old):
        is_sample = i - 1 >= N_PROMPT_TILES
        pre = prepare(params)
        for j in range(MIX_SUB):
            mixer(p_old, j, is_sample, st_ref, params, outs, scratch, pre)

    _two_stage_steps(i, N_MIX_TILES, p0, p1, fill, finish)


def _mixed_tile(i):
    return jnp.maximum(i - 1, 0)


def _sample_tile(i):
    return jnp.clip(i - 1 - N_PROMPT_TILES, 0, N_SAMPLE_TILES - 1)


def _mix_block(x, w, layer, state, params, param_specs, out_shapes, out_specs, scratch_shapes,
               prepare, mixer, name):
    k, n = w.shape[-2:]
    last = N_MIX_TILES - 1
    if isinstance(x, tuple):
        x_specs = [pl.BlockSpec((MIX_TM, k), lambda i: (jnp.minimum(i, N_PROMPT_TILES - 1), 0)),
                   pl.BlockSpec((MIX_TM, k), lambda i: (jnp.clip(i - N_PROMPT_TILES, 0, N_SAMPLE_TILES - 1), 0))]
    else:
        x_specs = [pl.BlockSpec((MIX_TM, k), lambda i: (jnp.minimum(i, last), 0))]
        x = (x,)
    state_spec = pl.BlockSpec((MIX_SUB,) + state.shape[1:], lambda i: (_sample_tile(i), 0, 0))
    return pl.pallas_call(
        functools.partial(_mix_block_body, n_x=len(x), n_params=len(params), n_outs=len(out_shapes),
                          layer=layer, prepare=prepare, mixer=mixer),
        grid=(N_MIX_TILES + 1,),
        in_specs=x_specs + [state_spec] + param_specs + [pl.BlockSpec(memory_space=pl.ANY)],
        out_specs=out_specs,
        out_shape=out_shapes,
        scratch_shapes=[pltpu.VMEM((k, n), BF16),
                        pltpu.VMEM((2, W_STAGE_ROWS, n), F32),
                        pltpu.VMEM((MIX_TM, n), F32),
                        pltpu.VMEM((MIX_TM, n), F32),
                        pltpu.SemaphoreType.DMA((2,))] + scratch_shapes,
        compiler_params=_cparams(1),
        name=name,
    )(*x, state, *params, w)


def _mix_out_specs(state_shape):
    return [pl.BlockSpec((MIX_TM, D), lambda i: (_mixed_tile(i), 0)),
            pl.BlockSpec((None,) + state_shape, lambda i: (0, 0, 0)),
            pl.BlockSpec((MIX_SUB,) + state_shape, lambda i: (_sample_tile(i), 0, 0))]


def _const_spec(shape):
    return pl.BlockSpec(shape, lambda i: (0,) * len(shape))


def _mix0_prepare(params):
    ws_ref = params[3]
    row = lax.broadcasted_iota(jnp.int32, (2 * STEP, 2 * STEP), 0)
    col = lax.broadcasted_iota(jnp.int32, (2 * STEP, 2 * STEP), 1)
    mask = (col // STEP) <= (row // STEP)
    return [jnp.where(mask, ws_ref[h], 0.0).astype(BF16) for h in range(SGU_HEADS)]


def _mix0_sub(p_ref, j, is_sample, st_ref, params, outs, scratch, w_masked):
    wc_ref, g_ref, b_ref, _, bs_ref = params
    y_ref, c3p_ref, c3s_ref, vrow_ref = outs
    ctx_ref, vprev_ref = scratch
    rows = slice(j * STEP, (j + 1) * STEP)

    z = p_ref[rows, 2 * D_A:3 * D_A] * p_ref[rows, 0:D_A]
    ctx_ref[6:8, :] = jnp.where(is_sample, _conv_operand(st_ref[j]), ctx_ref[6:8, :])
    zc = _conv_operand(z)
    ctx_ref[8:8 + STEP, :] = zc
    conv = (wc_ref[0:1, :] * ctx_ref[6:6 + STEP, :]
            + wc_ref[1:2, :] * ctx_ref[7:7 + STEP, :]
            + wc_ref[2:3, :] * zc)
    y_ref[rows, 0:D_A] = (p_ref[rows, D_A:2 * D_A] * conv).astype(y_ref.dtype)
    tail = z[STEP - 2:STEP, :]
    c3s_ref[j] = tail
    if j == MIX_SUB - 1:
        c3p_ref[...] = jnp.where(is_sample, c3p_ref[...], tail)
    ctx_ref[0:8, :] = ctx_ref[STEP:STEP + 8, :]

    u = jax.nn.gelu(p_ref[rows, 3 * D_A:3 * D_A + D_B])
    v = _ln(jax.nn.gelu(p_ref[rows, 3 * D_A + D_B:3 * D_A + 2 * D_B]), g_ref[...], b_ref[...])
    vrow_ref[j] = v
    vb = v.astype(BF16)
    for h in range(SGU_HEADS):
        hs = slice(h * 128, (h + 1) * 128)
        wm = w_masked[h]
        v_h = vb[:, hs]
        f = jnp.dot(wm[0:STEP, 0:STEP], v_h, preferred_element_type=F32) + bs_ref[h, 0:STEP, :]
        if j % 2 == 1:
            f_second = (jnp.dot(wm[STEP:, 0:STEP], vprev_ref[:, hs], preferred_element_type=F32)
                        + jnp.dot(wm[STEP:, STEP:], v_h, preferred_element_type=F32)
                        + bs_ref[h, STEP:, :])
            f = jnp.where(is_sample, f, f_second)
        y_ref[rows, D_A + h * 128:D_A + (h + 1) * 128] = (u[:, hs] * f).astype(y_ref.dtype)
    vprev_ref[...] = vb


def _mix0(x, w_in, layer, state_c3, w_conv3, g, b, w_spatial, b_spatial_col):
    params = [w_conv3, g, b, w_spatial, b_spatial_col]
    param_specs = [_const_spec((3, D_A)), _const_spec((1, D_B)), _const_spec((1, D_B)),
                   _const_spec((SGU_HEADS, 128, 128)), _const_spec((SGU_HEADS, 128, 1))]
    out_shapes = [jax.ShapeDtypeStruct((M_TOK, D), BF16),
                  jax.ShapeDtypeStruct((1, 2, D_A), F32),
                  jax.ShapeDtypeStruct((N_STREAMS, 2, D_A), F32),
                  jax.ShapeDtypeStruct((N_STREAMS, STEP, D_B), F32)]
    out_specs = _mix_out_specs((2, D_A)) + [
        pl.BlockSpec((MIX_SUB, STEP, D_B), lambda i: (_sample_tile(i), 0, 0))]
    scratch = [pltpu.VMEM((STEP + 8, D_A), F32), pltpu.VMEM((STEP, D_B), BF16)]
    return _mix_block(x, w_in, layer, state_c3, params, param_specs, out_shapes, out_specs, scratch,
                      _mix0_prepare, _mix0_sub, "mix_even")


CONF_W = 31
CTX_PAD = 32
CONV_COLS = 256
SUBLANES = 8
SHIFT_ROWS = CTX_PAD + STEP - SUBLANES


def _mix1_sub(p_ref, j, is_sample, st_ref, params, outs, scratch, _):
    w_ref, bdw_ref, g_ref, b_ref = params
    y_ref, c31p_ref, c31s_ref = outs
    ctx_ref, cv_ref, sh_ref = scratch
    rows = slice(j * STEP, (j + 1) * STEP)
    first = CTX_PAD - (CONF_W - 1)

    glu = p_ref[rows, 0:D] * jax.nn.sigmoid(p_ref[rows, D:2 * D])
    ctx_ref[first:CTX_PAD, :] = jnp.where(is_sample, _conv_operand(st_ref[j]), ctx_ref[first:CTX_PAD, :])
    ctx_ref[CTX_PAD:CTX_PAD + STEP, :] = _conv_operand(glu)
    tail = glu[STEP - (CONF_W - 1):STEP, :]
    c31s_ref[j] = tail
    if j == MIX_SUB - 1:
        c31p_ref[...] = jnp.where(is_sample, c31p_ref[...], tail)
    for c in range(D // CONV_COLS):
        cs = slice(c * CONV_COLS, (c + 1) * CONV_COLS)
        for s in range(1, SUBLANES):
            sh_ref[s - 1, :, :] = ctx_ref[s:s + SHIFT_ROWS, cs]
        acc = None
        for k in range(CONF_W):
            q, s = divmod(first + k, SUBLANES)
            taps = slice(q * SUBLANES, q * SUBLANES + STEP)
            operand = ctx_ref[taps, cs] if s == 0 else sh_ref[s - 1, taps, :]
            term = w_ref[k:k + 1, cs] * operand
            acc = term if acc is None else acc + term
        cv_ref[:, cs] = acc + bdw_ref[:, cs]
    c = _ln(cv_ref[...], g_ref[...], b_ref[...])
    y_ref[rows, :] = (c * jax.nn.sigmoid(c)).astype(y_ref.dtype)
    ctx_ref[0:CTX_PAD, :] = ctx_ref[STEP:STEP + CTX_PAD, :]


def _mix1(x, w_pw1, layer, state_c31, w_dw, b_dw, g, b):
    params = [w_dw, b_dw, g, b]
    param_specs = [_const_spec((CONF_W, D)), _const_spec((1, D)), _const_spec((1, D)), _const_spec((1, D))]
    out_shapes = [jax.ShapeDtypeStruct((M_TOK, D), BF16),
                  jax.ShapeDtypeStruct((1, CONF_W - 1, D), F32),
                  jax.ShapeDtypeStruct((N_STREAMS, CONF_W - 1, D), F32)]
    scratch = [pltpu.VMEM((CTX_PAD + STEP, D), F32),
               pltpu.VMEM((STEP, D), F32),
               pltpu.VMEM((SUBLANES - 1, SHIFT_ROWS, CONV_COLS), F32)]
    return _mix_block(x, w_pw1, layer, state_c31, params, param_specs, out_shapes,
                      _mix_out_specs((CONF_W - 1, D)), scratch, lambda params: None, _mix1_sub, "mix_odd")


ATTN_TM_PROMPT = 512
ATTN_TM_SAMPLE = 128


def _attn_block_body(x_ref, k_ref, v_ref, g_ref, b_ref, wq_hbm, wo_hbm, *rest,
                     layer, n_tiles, streams):
    o_ref, wq_b, wo_b, stage, acc0, acc1, x_prev, q_scr, a_scr, sem = rest[-10:]
    i = pl.program_id(0)
    rows_per_stream = x_ref.shape[0] // streams

    @pl.when(i == 0)
    def _():
        acc1[...] = jnp.zeros_like(acc1)
        x_prev[...] = jnp.zeros_like(x_prev)
        _load_weight_bf16(wq_hbm.at[layer], wq_b, stage, sem)
        _load_weight_bf16(wo_hbm.at[layer], wo_b, stage, sem)

    def fill(acc_new):
        q_scr[...] = jnp.dot(x_ref[...].astype(BF16), wq_b[...], preferred_element_type=F32).astype(BF16)
        for s in range(streams):
            rows = slice(s * rows_per_stream, (s + 1) * rows_per_stream)
            for h in range(HEADS):
                hs = slice(h * HEAD_DIM, (h + 1) * HEAD_DIM)
                sc = lax.dot_general(q_scr[rows, hs], k_ref[s, :, hs].astype(BF16),
                                     (((1,), (1,)), ((), ())), preferred_element_type=F32) * ATTN_SCALE
                p = jnp.exp(sc - jnp.max(sc, axis=-1, keepdims=True))
                p = p / jnp.sum(p, axis=-1, keepdims=True)
                a_scr[rows, hs] = _bdot(p, v_ref[s, :, hs]).astype(BF16)
        acc_new[...] = jnp.dot(a_scr[...], wo_b[...], preferred_element_type=F32)

    def finish(acc_old):
        o_ref[...] = _ln(ALPHA * x_prev[...] + acc_old[...], g_ref[...], b_ref[...])
        x_prev[...] = x_ref[...]

    _two_stage_steps(i, n_tiles, acc0, acc1, fill, finish)

    @pl.when(i > n_tiles)
    def _():
        o_ref[...] = jnp.zeros_like(o_ref)


def _attn_block_call(x, k, v, g, b, w_q, w_o, layer, *, tm, streams, first_tile, n_tiles, n_clear,
                     prev_out, name):
    last = n_tiles - 1

    def in_tile(i):
        return (first_tile + jnp.minimum(i, last), 0)

    def kv_block(i):
        return (layer, jnp.minimum(i, last) if k.shape[1] > streams else 0, 0, 0)

    def out_tile(i):
        return (first_tile + jnp.maximum(i - 1, 0), 0)

    in_specs = [pl.BlockSpec((tm, D), in_tile),
                pl.BlockSpec((None, streams, N_MEM, D), kv_block),
                pl.BlockSpec((None, streams, N_MEM, D), kv_block),
                pl.BlockSpec((1, D), lambda i: (0, 0)),
                pl.BlockSpec((1, D), lambda i: (0, 0)),
                pl.BlockSpec(memory_space=pl.ANY),
                pl.BlockSpec(memory_space=pl.ANY)]
    args = [x, k, v, g, b, w_q, w_o]
    aliases = {}
    if prev_out is not None:
        in_specs.append(pl.BlockSpec(memory_space=pl.ANY))
        args.append(prev_out)
        aliases = {len(args) - 1: 0}
    return pl.pallas_call(
        functools.partial(_attn_block_body, layer=layer, n_tiles=n_tiles, streams=streams),
        grid=(n_tiles + 1 + n_clear,),
        in_specs=in_specs,
        out_specs=pl.BlockSpec((tm, D), out_tile),
        out_shape=jax.ShapeDtypeStruct((M_TOK, D), F32),
        scratch_shapes=[pltpu.VMEM((D, D), BF16),
                        pltpu.VMEM((D, D), BF16),
                        pltpu.VMEM((2, W_STAGE_ROWS, D), F32),
                        pltpu.VMEM((tm, D), F32),
                        pltpu.VMEM((tm, D), F32),
                        pltpu.VMEM((tm, D), F32),
                        pltpu.VMEM((tm, D), BF16),
                        pltpu.VMEM((tm, D), BF16),
                        pltpu.SemaphoreType.DMA((2,))],
        input_output_aliases=aliases,
        compiler_params=_cparams(1),
        name=name,
    )(*args)


def _attn_block(x, pk, pv, ck, cv, g, b, w_q, w_o, layer):
    n_sample = N_STREAMS * STEP
    out = _attn_block_call(x, pk, pv, g, b, w_q, w_o, layer, tm=ATTN_TM_PROMPT, streams=1,
                           first_tile=0, n_tiles=N_PROMPT // ATTN_TM_PROMPT,
                           n_clear=n_sample // ATTN_TM_PROMPT, prev_out=None, name="attn_prompt")
    return _attn_block_call(x, ck, cv, g, b, w_q, w_o, layer, tm=ATTN_TM_SAMPLE,
                            streams=ATTN_TM_SAMPLE // STEP, first_tile=N_PROMPT // ATTN_TM_SAMPLE,
                            n_tiles=n_sample // ATTN_TM_SAMPLE, n_clear=0, prev_out=out,
                            name="attn_sample")


TOP_K = 2
ROUTE_TM = 512
EXP_TM = 512
N_ASSIGN = M_TOK * TOP_K
N_EXP_TILES = N_ASSIGN // EXP_TM
N_VISITS = N_EXP_TILES + N_EXPERTS - 1
MOVE_TM = 256
LANE = 128
CHUNKS = D // LANE
ISSUE_UNROLL = 8


def _to_chunk_major(dst_ref, x, rows):
    for c in range(CHUNKS):
        dst_ref[pl.ds(c, rows, stride=CHUNKS), :] = x[:, c * LANE:(c + 1) * LANE]


def _from_chunk_major(src_ref, rows):
    return jnp.concatenate([src_ref[pl.ds(c, rows, stride=CHUNKS), :] for c in range(CHUNKS)], axis=-1)


def _route_body(x_ref, wr_ref, br_ref, idx_ref, wts_ref, cnt_ref, run_ref):
    i = pl.program_id(0)

    @pl.when(i == 0)
    def _():
        run_ref[...] = jnp.zeros_like(run_ref)

    logits = _bdot(x_ref[...], wr_ref[...]) + br_ref[...]
    lane = lax.broadcasted_iota(jnp.int32, logits.shape, 1)

    def first_argmax(vals):
        top = jnp.max(vals, axis=-1, keepdims=True)
        idx = jnp.min(jnp.where(vals == top, lane, ROUTE_LANES), axis=-1, keepdims=True)
        return top, idx

    g_logits = jnp.where(lane < N_GROUPS, logits, NEG)
    g_top, g_idx = first_argmax(g_logits)
    p_group = 1.0 / jnp.sum(jnp.exp(g_logits - g_top), axis=-1, keepdims=True)
    lo_lane = N_GROUPS + E_PER * g_idx
    e_logits = jnp.where(jnp.logical_and(lane >= lo_lane, lane < lo_lane + E_PER), logits, NEG)
    v1, i1 = first_argmax(e_logits)
    v2, i2 = first_argmax(jnp.where(lane == i1, NEG, e_logits))
    t = jnp.exp(v2 - v1)
    w1 = p_group / (1.0 + t)
    w2 = p_group * t / (1.0 + t)

    a1 = (lane == i1).astype(F32)
    a2 = (lane == i2).astype(F32)
    a = a1 + a2
    tm = logits.shape[0]
    earlier = (lax.broadcasted_iota(jnp.int32, (tm, tm), 1)
               < lax.broadcasted_iota(jnp.int32, (tm, tm), 0)).astype(BF16)
    before = jnp.dot(earlier, a.astype(BF16), preferred_element_type=F32) + run_ref[...]
    r1 = jnp.sum(a1 * before, axis=-1, keepdims=True)
    r2 = jnp.sum(a2 * before, axis=-1, keepdims=True)
    run_ref[...] += jnp.sum(a, axis=0, keepdims=True)
    cnt_ref[...] = run_ref[...].astype(jnp.int32)

    e1 = (i1 - N_GROUPS).astype(F32)
    e2 = (i2 - N_GROUPS).astype(F32)
    packed = (jnp.where(lane == 0, e1, 0.0) + jnp.where(lane == 1, e2, 0.0)
              + jnp.where(lane == 2, r1, 0.0) + jnp.where(lane == 3, r2, 0.0))
    idx_ref[...] = packed.T[0:8, :].astype(jnp.int32)
    wts_ref[...] = jnp.where(lane == 0, w1, 0.0) + jnp.where(lane == 1, w2, 0.0)


def _route(x, wr, br):
    m = x.shape[0]
    tm = ROUTE_TM
    return pl.pallas_call(
        _route_body,
        grid=(m // tm,),
        in_specs=[pl.BlockSpec((tm, D), lambda i: (i, 0)),
                  pl.BlockSpec((D, ROUTE_LANES), lambda i: (0, 0)),
                  pl.BlockSpec((1, ROUTE_LANES), lambda i: (0, 0))],
        out_specs=[pl.BlockSpec((8, tm), lambda i: (0, i)),
                   pl.BlockSpec((tm, ROUTE_LANES), lambda i: (i, 0)),
                   pl.BlockSpec((1, ROUTE_LANES), lambda i: (0, 0))],
        out_shape=[jax.ShapeDtypeStruct((8, m), jnp.int32),
                   jax.ShapeDtypeStruct((m, ROUTE_LANES), F32),
                   jax.ShapeDtypeStruct((1, ROUTE_LANES), jnp.int32)],
        scratch_shapes=[pltpu.VMEM((1, ROUTE_LANES), F32)],
        compiler_params=_cparams(1),
        name="moe_route",
    )(x, wr, br)


def _plan_body(cnt_ref, idx_ref, pos_ref, vt_ref, ve_ref, vlo_ref, vhi_ref, nvis_ref):
    shift = EXP_TM.bit_length() - 1
    e1 = idx_ref[0:1, :]
    e2 = idx_ref[1:2, :]
    pos1 = idx_ref[2:3, :]
    pos2 = idx_ref[3:4, :]
    start = jnp.int32(0)
    n_vis = jnp.int32(0)
    last_e = jnp.int32(0)
    for e in range(N_EXPERTS):
        cnt = cnt_ref[N_GROUPS + e]
        end = start + cnt
        pos1 = pos1 + jnp.where(e1 == e, start, 0)
        pos2 = pos2 + jnp.where(e2 == e, start, 0)
        first_tile = lax.shift_right_logical(start, shift)
        n_tiles = jnp.where(cnt > 0, lax.shift_right_logical(end + (EXP_TM - 1), shift) - first_tile, 0)

        def add_visit(k, carry, e=e, start=start, end=end, first_tile=first_tile, n_vis=n_vis):
            tile = first_tile + k
            vt_ref[n_vis + k] = tile
            ve_ref[n_vis + k] = e
            vlo_ref[n_vis + k] = jnp.maximum(start - tile * EXP_TM, 0)
            vhi_ref[n_vis + k] = jnp.minimum(end - tile * EXP_TM, EXP_TM)
            return carry

        lax.fori_loop(0, n_tiles, add_visit, 0)
        n_vis = n_vis + n_tiles
        last_e = jnp.where(cnt > 0, e, last_e)
        start = end
    nvis_ref[0] = n_vis

    def add_idle(k, carry):
        vt_ref[k] = N_EXP_TILES - 1
        ve_ref[k] = last_e
        vlo_ref[k] = 0
        vhi_ref[k] = 0
        return carry

    lax.fori_loop(n_vis, N_VISITS, add_idle, 0)
    pos_ref[...] = jnp.concatenate([pos1, pos2, jnp.zeros((6, M_TOK), jnp.int32)], axis=0)


def _plan(cnt, idx):
    smem = pl.BlockSpec(memory_space=pltpu.SMEM)
    vmem = pl.BlockSpec(memory_space=pltpu.VMEM)
    table = jax.ShapeDtypeStruct((N_VISITS,), jnp.int32)
    return pl.pallas_call(
        _plan_body,
        in_specs=[smem, vmem],
        out_specs=[vmem, smem, smem, smem, smem, smem],
        out_shape=[jax.ShapeDtypeStruct((8, M_TOK), jnp.int32), table, table, table, table,
                   jax.ShapeDtypeStruct((1,), jnp.int32)],
        name="moe_plan",
    )(cnt, idx)


def _row_copy(src_ref, src_row, dst_ref, dst_row, sem):
    return pltpu.make_async_copy(src_ref.at[pl.ds(src_row * CHUNKS, CHUNKS)],
                                 dst_ref.at[pl.ds(dst_row * CHUNKS, CHUNKS)], sem)


def _scatter_body(pos1_ref, pos2_ref, x_ref, xs_hbm, stage, sem):
    i = pl.program_id(0)
    last = pl.num_programs(0) - 1
    slot = i % 2

    def wait_slot(s):
        for _ in range(TOP_K):
            pltpu.make_async_copy(stage.at[s], xs_hbm.at[pl.ds(0, MOVE_TM * CHUNKS)], sem.at[s]).wait()

    @pl.when(i >= 2)
    def _():
        wait_slot(slot)

    _to_chunk_major(stage.at[slot], x_ref[...], MOVE_TM)

    def issue(r, carry):
        t = i * MOVE_TM + r
        _row_copy(stage.at[slot], r, xs_hbm, pos1_ref[t], sem.at[slot]).start()
        _row_copy(stage.at[slot], r, xs_hbm, pos2_ref[t], sem.at[slot]).start()
        return carry

    lax.fori_loop(0, MOVE_TM, issue, 0, unroll=ISSUE_UNROLL)

    @pl.when(i == last)
    def _():
        wait_slot(1 - slot)
        wait_slot(slot)


def _scatter(pos1, pos2, x):
    m = x.shape[0]
    assert (m // MOVE_TM) >= 2
    grid_spec = pltpu.PrefetchScalarGridSpec(
        num_scalar_prefetch=2,
        grid=(m // MOVE_TM,),
        in_specs=[pl.BlockSpec((MOVE_TM, D), lambda i, p1, p2: (i, 0))],
        out_specs=pl.BlockSpec(memory_space=pl.ANY),
        scratch_shapes=[pltpu.VMEM((2, MOVE_TM * CHUNKS, LANE), F32),
                        pltpu.SemaphoreType.DMA((2,))])
    return pl.pallas_call(
        _scatter_body,
        grid_spec=grid_spec,
        out_shape=jax.ShapeDtypeStruct((N_ASSIGN * CHUNKS, LANE), F32),
        compiler_params=_cparams(1),
        name="moe_scatter",
    )(pos1, pos2, x)


def _experts_body(vt_ref, ve_ref, vlo_ref, vhi_ref, nvis_ref, x_ref, wg_ref, wu_ref, wd_ref,
                  o_ref, wg_b, wu_b, wd_b):
    v = pl.program_id(0)
    prev = jnp.maximum(v - 1, 0)

    @pl.when(v < nvis_ref[0])
    def _():
        @pl.when(jnp.logical_or(v == 0, ve_ref[v] != ve_ref[prev]))
        def _():
            wg_b[...] = wg_ref[...].astype(BF16)
            wu_b[...] = wu_ref[...].astype(BF16)
            wd_b[...] = wd_ref[...].astype(BF16)

        xb = _from_chunk_major(x_ref, EXP_TM).astype(BF16)
        gate = jnp.dot(xb, wg_b[...], preferred_element_type=F32)
        up = jnp.dot(xb, wu_b[...], preferred_element_type=F32)
        h = (gate * jax.nn.sigmoid(gate) * up).astype(BF16)
        y = jnp.dot(h, wd_b[...], preferred_element_type=F32)
        row = lax.broadcasted_iota(jnp.int32, (EXP_TM, 1), 0)
        mine = jnp.logical_and(row >= vlo_ref[v], row < vhi_ref[v])
        first_visit = jnp.logical_or(v == 0, vt_ref[v] != vt_ref[prev])

        @pl.when(first_visit)
        def _():
            _to_chunk_major(o_ref, jnp.where(mine, y, 0.0), EXP_TM)

        @pl.when(jnp.logical_not(first_visit))
        def _():
            _to_chunk_major(o_ref, jnp.where(mine, y, _from_chunk_major(o_ref, EXP_TM)), EXP_TM)


def _experts(vt, ve, vlo, vhi, nvis, xs, w_gate, w_up, w_down, layer):
    def tile_map(v, vt, ve, vlo, vhi, nvis):
        return (vt[v], 0)

    def weight_map(v, vt, ve, vlo, vhi, nvis):
        return (layer, ve[v], 0, 0)

    grid_spec = pltpu.PrefetchScalarGridSpec(
        num_scalar_prefetch=5,
        grid=(N_VISITS,),
        in_specs=[pl.BlockSpec((EXP_TM * CHUNKS, LANE), tile_map),
                  pl.BlockSpec((None, None, D, D_EXPERT), weight_map),
                  pl.BlockSpec((None, None, D, D_EXPERT), weight_map),
                  pl.BlockSpec((None, None, D_EXPERT, D), weight_map)],
        out_specs=pl.BlockSpec((EXP_TM * CHUNKS, LANE), tile_map),
        scratch_shapes=[pltpu.VMEM((D, D_EXPERT), BF16),
                        pltpu.VMEM((D, D_EXPERT), BF16),
                        pltpu.VMEM((D_EXPERT, D), BF16)])
    return pl.pallas_call(
        _experts_body,
        grid_spec=grid_spec,
        out_shape=jax.ShapeDtypeStruct((N_ASSIGN * CHUNKS, LANE), F32),
        compiler_params=_cparams(1),
        name="moe_experts",
    )(vt, ve, vlo, vhi, nvis, xs, w_gate, w_up, w_down)


def _combine_body(pos1_ref, pos2_ref, x_ref, wts_ref, g_ref, b_ref, ys_hbm, *rest, split):
    out_refs, (ybuf, sem) = rest[:-2], rest[-2:]
    i = pl.program_id(0)
    slot = i % 2

    def gather_rows(step, s):
        def issue(r, carry):
            t = step * MOVE_TM + r
            _row_copy(ys_hbm, pos1_ref[t], ybuf.at[s, 0], r, sem.at[s]).start()
            _row_copy(ys_hbm, pos2_ref[t], ybuf.at[s, 1], r, sem.at[s]).start()
            return carry
        lax.fori_loop(0, MOVE_TM, issue, 0, unroll=ISSUE_UNROLL)

    @pl.when(i == 0)
    def _():
        gather_rows(0, 0)

    @pl.when(i + 1 < pl.num_programs(0))
    def _():
        gather_rows(i + 1, 1 - slot)

    for k in range(TOP_K):
        pltpu.make_async_copy(ys_hbm.at[pl.ds(0, MOVE_TM * CHUNKS)], ybuf.at[slot, k], sem.at[slot]).wait()
    y = (wts_ref[:, 0:1] * _from_chunk_major(ybuf.at[slot, 0], MOVE_TM)
         + wts_ref[:, 1:2] * _from_chunk_major(ybuf.at[slot, 1], MOVE_TM))
    out = _ln(ALPHA * x_ref[...] + y, g_ref[...], b_ref[...])
    if split is None:
        out_refs[0][...] = out
    else:
        @pl.when(i < split)
        def _():
            out_refs[0][...] = out

        @pl.when(i >= split)
        def _():
            out_refs[1][...] = out


def _combine(pos1, pos2, x, wts, g, b, ys, split_rows=None):
    m = x.shape[0]
    if split_rows is None:
        split = None
        out_specs = pl.BlockSpec((MOVE_TM, D), lambda i, p1, p2: (i, 0))
        out_shape = jax.ShapeDtypeStruct((m, D), F32)
    else:
        split = split_rows // MOVE_TM
        out_specs = [pl.BlockSpec((MOVE_TM, D), lambda i, p1, p2: (jnp.minimum(i, split - 1), 0)),
                     pl.BlockSpec((MOVE_TM, D), lambda i, p1, p2: (jnp.maximum(i - split, 0), 0))]
        out_shape = [jax.ShapeDtypeStruct((split_rows, D), F32),
                     jax.ShapeDtypeStruct((m - split_rows, D), F32)]
    grid_spec = pltpu.PrefetchScalarGridSpec(
        num_scalar_prefetch=2,
        grid=(m // MOVE_TM,),
        in_specs=[pl.BlockSpec((MOVE_TM, D), lambda i, p1, p2: (i, 0)),
                  pl.BlockSpec((MOVE_TM, ROUTE_LANES), lambda i, p1, p2: (i, 0)),
                  pl.BlockSpec((1, D), lambda i, p1, p2: (0, 0)),
                  pl.BlockSpec((1, D), lambda i, p1, p2: (0, 0)),
                  pl.BlockSpec(memory_space=pl.ANY)],
        out_specs=out_specs,
        scratch_shapes=[pltpu.VMEM((2, TOP_K, MOVE_TM * CHUNKS, LANE), F32),
                        pltpu.SemaphoreType.DMA((2,))])
    return pl.pallas_call(
        functools.partial(_combine_body, split=split),
        grid_spec=grid_spec,
        out_shape=out_shape,
        compiler_params=_cparams(1),
        name="moe_combine",
    )(pos1, pos2, x, wts, g, b, ys)


def _moe(x, wr, br, w_gate, w_up, w_down, layer, g, b, split_rows=None):
    idx, wts, cnt = _route(x, wr, br)
    pos, vt, ve, vlo, vhi, nvis = _plan(cnt.reshape(ROUTE_LANES), idx)
    xs = _scatter(pos[0], pos[1], x)
    ys = _experts(vt, ve, vlo, vhi, nvis, xs, w_gate, w_up, w_down, layer)
    return _combine(pos[0], pos[1], x, wts, g, b, ys, split_rows)


def _router_params(w_rg, b_rg, w_re, b_re):
    w = jnp.concatenate([w_rg, jnp.transpose(w_re, (1, 0, 2)).reshape(D, N_EXPERTS)], axis=1)
    bias = jnp.concatenate([b_rg, b_re.reshape(N_EXPERTS)])
    pad = ROUTE_LANES - w.shape[1]
    return jnp.pad(w, ((0, 0), (0, pad))), jnp.pad(bias, (0, pad))[None, :]


def kernel(x_prompt, x_sample, mem_prompt, cache_mem_k, cache_mem_v, state_conv3, state_conv31,
           ln_gain, ln_bias, w_in_ab, w_conv3, ln_sgu_gain, ln_sgu_bias, w_spatial, b_spatial, w_out_ab,
           w_pw1, w_dw31, b_dw31, ln_conf_gain, ln_conf_bias, w_pw2,
           w_mem_q, w_mem_k, w_mem_v, w_mem_o,
           w_route_group, b_route_group, w_route_expert, b_route_expert,
           w_exp_gate, w_exp_up, w_exp_down):
    depth = ln_gain.shape[0]
    x = (x_prompt.reshape(N_PROMPT, D), x_sample.reshape(N_STREAMS * STEP, D))

    mem = mem_prompt.reshape(N_MEM, D)
    pk, pk_b = _mem_kv(mem, w_mem_k, "mem_k")
    pv, pv_b = _mem_kv(mem, w_mem_v, "mem_v")
    pk_b = pk_b.reshape(depth, 1, N_MEM, D)
    pv_b = pv_b.reshape(depth, 1, N_MEM, D)
    ck = cache_mem_k.reshape(depth, N_STREAMS, N_MEM, D)
    cv = cache_mem_v.reshape(depth, N_STREAMS, N_MEM, D)

    def ln_params(l, k):
        return ln_gain[l, k][None, :], ln_bias[l, k][None, :]

    c3_prompt, c3_sample, c31_prompt, c31_sample, v_rows = [], [], [], [], []
    for l in range(depth):
        i = l // 2
        if l % 2 == 0:
            y_mix, c3p, c3s, v = _mix0(
                x, w_in_ab, i, state_conv3[i], w_conv3[i], ln_sgu_gain[i][None, :], ln_sgu_bias[i][None, :],
                w_spatial[i], b_spatial[i][:, :, None])
            c3_prompt.append(c3p)
            c3_sample.append(c3s)
            v_rows.append(v)
            w_out = w_out_ab
        else:
            y_mix, c31p, c31s = _mix1(
                x, w_pw1, i, state_conv31[i], w_dw31[i], b_dw31[i][None, :],
                ln_conf_gain[i][None, :], ln_conf_bias[i][None, :])
            c31_prompt.append(c31p)
            c31_sample.append(c31s)
            w_out = w_pw2
        x = _mm_ln(y_mix, w_out, i, x, *ln_params(l, 0), name="mix_out_ln")
        x = _attn_block(x, pk_b, pv_b, ck, cv, *ln_params(l, 1), w_mem_q, w_mem_o, l)
        wr, br = _router_params(w_route_group[l], b_route_group[l], w_route_expert[l], b_route_expert[l])
        x = _moe(x, wr, br, w_exp_gate, w_exp_up, w_exp_down, l, *ln_params(l, 2),
                 split_rows=N_PROMPT if l == depth - 1 else None)

    y_prompt = x[0].reshape(1, N_PROMPT, D)
    y_sample = x[1].reshape(N_STREAMS, STEP, D)
    prompt_mem_k = pk.reshape(depth, 1, N_MEM, HEADS, HEAD_DIM)
    prompt_mem_v = pv.reshape(depth, 1, N_MEM, HEADS, HEAD_DIM)
    prompt_conv3 = jnp.stack(c3_prompt)
    sample_conv3 = jnp.stack(c3_sample)
    prompt_conv31 = jnp.stack(c31_prompt)
    sample_conv31 = jnp.stack(c31_sample)
    sample_sgu_v = jnp.stack(v_rows)
    return (y_prompt, y_sample, prompt_mem_k, prompt_mem_v, prompt_conv3, prompt_conv31,
            sample_conv3, sample_conv31, sample_sgu_v)
```

```python
import functools

import jax
import jax.numpy as jnp
from jax import lax
from jax.experimental import pallas as pl
from jax.experimental.pallas import tpu as pltpu

F32 = jnp.float32
BF16 = jnp.bfloat16

D = 2048
N_PROMPT = 8192
N_STREAMS = 16
STEP = 64
NP_STEPS = N_PROMPT // STEP
M_TOK = N_PROMPT + N_STREAMS * STEP
N_MEM = 256
HEADS = 4
HEAD_DIM = D // HEADS
D_A = 1024
D_B = 1024
SGU_HEADS = 8
N_GROUPS = 4
E_PER = 4
N_EXPERTS = 16
D_EXPERT = 512
LN_EPS = 1e-5
ALPHA = (2.0 * 2) ** 0.25
ATTN_SCALE = HEAD_DIM ** -0.5
ROUTE_LANES = 128
NEG = -1e30

VMEM_LIMIT_BYTES = 60000 * 1024


def _cparams(n_axes):
    return pltpu.CompilerParams(
        dimension_semantics=("arbitrary",) * n_axes,
        vmem_limit_bytes=VMEM_LIMIT_BYTES)


def _ln(x, g, b):
    mu = jnp.mean(x, axis=-1, keepdims=True)
    xc = x - mu
    var = jnp.mean(xc * xc, axis=-1, keepdims=True)
    return xc * lax.rsqrt(var + LN_EPS) * g + b


def _bdot(a, b):
    return jnp.dot(a.astype(BF16), b.astype(BF16), preferred_element_type=F32)


def _conv_operand(x):
    return x.astype(BF16).astype(F32)


def _mem_kv_body(x_ref, w_ref, o_ref, ob_ref):
    y = _bdot(x_ref[...], w_ref[...])
    o_ref[...] = y
    ob_ref[...] = y.astype(BF16)


def _mem_kv(mem, w, name):
    depth, k, n = w.shape
    tn = 512
    out_block = pl.BlockSpec((None, N_MEM, tn), lambda l, j: (l, 0, j))
    return pl.pallas_call(
        _mem_kv_body,
        grid=(depth, n // tn),
        in_specs=[pl.BlockSpec((N_MEM, k), lambda l, j: (0, 0)),
                  pl.BlockSpec((None, k, tn), lambda l, j: (l, 0, j))],
        out_specs=[out_block, out_block],
        out_shape=[jax.ShapeDtypeStruct((depth, N_MEM, n), F32),
                   jax.ShapeDtypeStruct((depth, N_MEM, n), BF16)],
        compiler_params=_cparams(2),
        name=name,
    )(mem, w)


MM_LN_TM = 512
W_STAGE_ROWS = 128


def _load_weight_bf16(w_hbm, wb, stage, sem):
    rows = stage.shape[1]
    n_chunks = wb.shape[0] // rows

    def copy(k):
        return pltpu.make_async_copy(w_hbm.at[pl.ds(k * rows, rows)], stage.at[k % 2], sem.at[k % 2])

    copy(0).start()
    for k in range(n_chunks):
        if k + 1 < n_chunks:
            copy(k + 1).start()
        copy(k).wait()
        wb[k * rows:(k + 1) * rows, :] = stage[k % 2].astype(BF16)


def _two_stage_steps(i, n_tiles, buf0, buf1, fill, finish, separate_drain=True):
    last_filling = n_tiles if separate_drain else n_tiles + 1

    @pl.when(jnp.logical_and(i < last_filling, i % 2 == 0))
    def _():
        fill(buf0)
        finish(buf1)

    @pl.when(jnp.logical_and(i < last_filling, i % 2 == 1))
    def _():
        fill(buf1)
        finish(buf0)

    if separate_drain:
        @pl.when(i == n_tiles)
        def _():
            finish(buf1 if n_tiles % 2 == 0 else buf0)


def _mm_ln_body(x_ref, *rest, layer, res_split, n_tiles):
    if res_split is None:
        r_ref, g_ref, b_ref, w_hbm, o_ref, wb, stage, acc0, acc1, sem = rest
    else:
        ra_ref, rb_ref, g_ref, b_ref, w_hbm, o_ref, wb, stage, acc0, acc1, sem = rest
    i = pl.program_id(0)

    @pl.when(i == 0)
    def _():
        acc1[...] = jnp.zeros_like(acc1)
        _load_weight_bf16(w_hbm.at[layer], wb, stage, sem)

    def fill(acc):
        acc[...] = jnp.dot(x_ref[...], wb[...], preferred_element_type=F32)

    def finish(acc):
        if res_split is None:
            res = r_ref[...]
        else:
            res = jnp.where(i - 1 < res_split, ra_ref[...], rb_ref[...])
        o_ref[...] = _ln(ALPHA * res + acc[...], g_ref[...], b_ref[...])

    _two_stage_steps(i, n_tiles, acc0, acc1, fill, finish)


def _mm_ln(x, w, layer, res, g, b, name):
    m, k = x.shape
    n = w.shape[-1]
    tm = MM_LN_TM
    n_tiles = m // tm
    assert x.dtype == BF16

    def done_tile(i):
        return (jnp.maximum(i - 1, 0), 0)

    if isinstance(res, tuple):
        res_split = res[0].shape[0] // tm
        res_specs = [pl.BlockSpec((tm, n), lambda i: (jnp.clip(i - 1, 0, res_split - 1), 0)),
                     pl.BlockSpec((tm, n), lambda i: (jnp.maximum(i - 1 - res_split, 0), 0))]
    else:
        res_split = None
        res_specs = [pl.BlockSpec((tm, n), done_tile)]
        res = (res,)

    return pl.pallas_call(
        functools.partial(_mm_ln_body, layer=layer, res_split=res_split, n_tiles=n_tiles),
        grid=(n_tiles + 1,),
        in_specs=[pl.BlockSpec((tm, k), lambda i: (jnp.minimum(i, n_tiles - 1), 0))] + res_specs + [
                  pl.BlockSpec((1, n), lambda i: (0, 0)),
                  pl.BlockSpec((1, n), lambda i: (0, 0)),
                  pl.BlockSpec(memory_space=pl.ANY)],
        out_specs=pl.BlockSpec((tm, n), done_tile),
        out_shape=jax.ShapeDtypeStruct((m, n), F32),
        scratch_shapes=[pltpu.VMEM((k, n), BF16),
                        pltpu.VMEM((2, W_STAGE_ROWS, n), F32),
                        pltpu.VMEM((tm, n), F32),
                        pltpu.VMEM((tm, n), F32),
                        pltpu.SemaphoreType.DMA((2,))],
        compiler_params=_cparams(1),
        name=name,
    )(x, *res, g, b, w)


MIX_TM = 256
MIX_SUB = MIX_TM // STEP
N_PROMPT_TILES = N_PROMPT // MIX_TM
N_MIX_TILES = M_TOK // MIX_TM
N_SAMPLE_TILES = N_MIX_TILES - N_PROMPT_TILES


def _mix_block_body(*refs, n_x, n_params, n_outs, layer, prepare, mixer, separate_drain):
    x_refs, refs = refs[:n_x], refs[n_x:]
    st_ref, refs = refs[0], refs[1:]
    params, refs = refs[:n_params], refs[n_params:]
    w_hbm, refs = refs[0], refs[1:]
    outs, refs = refs[:n_outs], refs[n_outs:]
    wb, stage, p0, p1, sem = refs[:5]
    scratch = refs[5:]
    i = pl.program_id(0)

    @pl.when(i == 0)
    def _():
        p1[...] = jnp.zeros_like(p1)
        for ref in tuple(scratch) + tuple(outs[1:2]):
            ref[...] = jnp.zeros_like(ref)
        _load_weight_bf16(w_hbm.at[layer], wb, stage, sem)

    def fill(p_new):
        if n_x == 1:
            x = x_refs[0][...]
        else:
            x = jnp.where(i < N_PROMPT_TILES, x_refs[0][...], x_refs[1][...])
        p_new[...] = jnp.dot(x.astype(BF16), wb[...], preferred_element_type=F32)

    def finish(p_old):
        is_sample = i - 1 >= N_PROMPT_TILES
        pre = prepare(params)
        for j in range(MIX_SUB):
            mixer(p_old, j, is_sample, st_ref, params, outs, scratch, pre)

    _two_stage_steps(i, N_MIX_TILES, p0, p1, fill, finish, separate_drain)


def _mixed_tile(i):
    return jnp.maximum(i - 1, 0)


def _sample_tile(i):
    return jnp.clip(i - 1 - N_PROMPT_TILES, 0, N_SAMPLE_TILES - 1)


def _mix_block(x, w, layer, state, params, param_specs, out_shapes, out_specs, scratch_shapes,
               prepare, mixer, name, separate_drain=True):
    k, n = w.shape[-2:]
    last = N_MIX_TILES - 1
    if isinstance(x, tuple):
        x_specs = [pl.BlockSpec((MIX_TM, k), lambda i: (jnp.minimum(i, N_PROMPT_TILES - 1), 0)),
                   pl.BlockSpec((MIX_TM, k), lambda i: (jnp.clip(i - N_PROMPT_TILES, 0, N_SAMPLE_TILES - 1), 0))]
    else:
        x_specs = [pl.BlockSpec((MIX_TM, k), lambda i: (jnp.minimum(i, last), 0))]
        x = (x,)
    state_spec = pl.BlockSpec((MIX_SUB,) + state.shape[1:], lambda i: (_sample_tile(i), 0, 0))
    return pl.pallas_call(
        functools.partial(_mix_block_body, n_x=len(x), n_params=len(params), n_outs=len(out_shapes),
                          layer=layer, prepare=prepare, mixer=mixer, separate_drain=separate_drain),
        grid=(N_MIX_TILES + 1,),
        in_specs=x_specs + [state_spec] + param_specs + [pl.BlockSpec(memory_space=pl.ANY)],
        out_specs=out_specs,
        out_shape=out_shapes,
        scratch_shapes=[pltpu.VMEM((k, n), BF16),
                        pltpu.VMEM((2, W_STAGE_ROWS, n), F32),
                        pltpu.VMEM((MIX_TM, n), F32),
                        pltpu.VMEM((MIX_TM, n), F32),
                        pltpu.SemaphoreType.DMA((2,))] + scratch_shapes,
        compiler_params=_cparams(1),
        name=name,
    )(*x, state, *params, w)


def _mix_out_specs(state_shape):
    return [pl.BlockSpec((MIX_TM, D), lambda i: (_mixed_tile(i), 0)),
            pl.BlockSpec((None,) + state_shape, lambda i: (0, 0, 0)),
            pl.BlockSpec((MIX_SUB,) + state_shape, lambda i: (_sample_tile(i), 0, 0))]


def _const_spec(shape):
    return pl.BlockSpec(shape, lambda i: (0,) * len(shape))


def _mix0_prepare(params):
    ws_ref = params[3]
    row = lax.broadcasted_iota(jnp.int32, (2 * STEP, 2 * STEP), 0)
    col = lax.broadcasted_iota(jnp.int32, (2 * STEP, 2 * STEP), 1)
    mask = (col // STEP) <= (row // STEP)
    return [jnp.where(mask, ws_ref[h], 0.0).astype(BF16) for h in range(SGU_HEADS)]


def _mix0_sub(p_ref, j, is_sample, st_ref, params, outs, scratch, w_masked):
    wc_ref, g_ref, b_ref, _, bs_ref = params
    y_ref, c3p_ref, c3s_ref, vrow_ref = outs
    ctx_ref, vprev_ref = scratch
    rows = slice(j * STEP, (j + 1) * STEP)

    z = p_ref[rows, 2 * D_A:3 * D_A] * p_ref[rows, 0:D_A]
    ctx_ref[6:8, :] = jnp.where(is_sample, _conv_operand(st_ref[j]), ctx_ref[6:8, :])
    zc = _conv_operand(z)
    ctx_ref[8:8 + STEP, :] = zc
    conv = (wc_ref[0:1, :] * ctx_ref[6:6 + STEP, :]
            + wc_ref[1:2, :] * ctx_ref[7:7 + STEP, :]
            + wc_ref[2:3, :] * zc)
    y_ref[rows, 0:D_A] = (p_ref[rows, D_A:2 * D_A] * conv).astype(y_ref.dtype)
    tail = z[STEP - 2:STEP, :]
    c3s_ref[j] = tail
    if j == MIX_SUB - 1:
        c3p_ref[...] = jnp.where(is_sample, c3p_ref[...], tail)
    ctx_ref[0:8, :] = ctx_ref[STEP:STEP + 8, :]

    u = jax.nn.gelu(p_ref[rows, 3 * D_A:3 * D_A + D_B])
    v = _ln(jax.nn.gelu(p_ref[rows, 3 * D_A + D_B:3 * D_A + 2 * D_B]), g_ref[...], b_ref[...])
    vrow_ref[j] = v
    vb = v.astype(BF16)
    for h in range(SGU_HEADS):
        hs = slice(h * 128, (h + 1) * 128)
        wm = w_masked[h]
        v_h = vb[:, hs]
        f = jnp.dot(wm[0:STEP, 0:STEP], v_h, preferred_element_type=F32) + bs_ref[h, 0:STEP, :]
        if j % 2 == 1:
            f_second = (jnp.dot(wm[STEP:, 0:STEP], vprev_ref[:, hs], preferred_element_type=F32)
                        + jnp.dot(wm[STEP:, STEP:], v_h, preferred_element_type=F32)
                        + bs_ref[h, STEP:, :])
            f = jnp.where(is_sample, f, f_second)
        y_ref[rows, D_A + h * 128:D_A + (h + 1) * 128] = (u[:, hs] * f).astype(y_ref.dtype)
    vprev_ref[...] = vb


def _mix0(x, w_in, layer, state_c3, w_conv3, g, b, w_spatial, b_spatial_col):
    params = [w_conv3, g, b, w_spatial, b_spatial_col]
    param_specs = [_const_spec((3, D_A)), _const_spec((1, D_B)), _const_spec((1, D_B)),
                   _const_spec((SGU_HEADS, 128, 128)), _const_spec((SGU_HEADS, 128, 1))]
    out_shapes = [jax.ShapeDtypeStruct((M_TOK, D), BF16),
                  jax.ShapeDtypeStruct((1, 2, D_A), F32),
                  jax.ShapeDtypeStruct((N_STREAMS, 2, D_A), F32),
                  jax.ShapeDtypeStruct((N_STREAMS, STEP, D_B), F32)]
    out_specs = _mix_out_specs((2, D_A)) + [
        pl.BlockSpec((MIX_SUB, STEP, D_B), lambda i: (_sample_tile(i), 0, 0))]
    scratch = [pltpu.VMEM((STEP + 8, D_A), F32), pltpu.VMEM((STEP, D_B), BF16)]
    return _mix_block(x, w_in, layer, state_c3, params, param_specs, out_shapes, out_specs, scratch,
                      _mix0_prepare, _mix0_sub, "mix_even")


CONF_W = 31
CTX_PAD = 32
CONV_COLS = 256
SUBLANES = 8
SHIFT_ROWS = CTX_PAD + STEP - SUBLANES


def _mix1_sub(p_ref, j, is_sample, st_ref, params, outs, scratch, _):
    w_ref, bdw_ref, g_ref, b_ref = params
    y_ref, c31p_ref, c31s_ref = outs
    ctx_ref, cv_ref, sh_ref = scratch
    rows = slice(j * STEP, (j + 1) * STEP)
    first = CTX_PAD - (CONF_W - 1)

    glu = p_ref[rows, 0:D] * jax.nn.sigmoid(p_ref[rows, D:2 * D])
    ctx_ref[first:CTX_PAD, :] = jnp.where(is_sample, _conv_operand(st_ref[j]), ctx_ref[first:CTX_PAD, :])
    ctx_ref[CTX_PAD:CTX_PAD + STEP, :] = _conv_operand(glu)
    tail = glu[STEP - (CONF_W - 1):STEP, :]
    c31s_ref[j] = tail
    if j == MIX_SUB - 1:
        c31p_ref[...] = jnp.where(is_sample, c31p_ref[...], tail)
    for c in range(D // CONV_COLS):
        cs = slice(c * CONV_COLS, (c + 1) * CONV_COLS)
        for s in range(1, SUBLANES):
            sh_ref[s - 1, :, :] = ctx_ref[s:s + SHIFT_ROWS, cs]
        acc = None
        for k in range(CONF_W):
            q, s = divmod(first + k, SUBLANES)
            taps = slice(q * SUBLANES, q * SUBLANES + STEP)
            operand = ctx_ref[taps, cs] if s == 0 else sh_ref[s - 1, taps, :]
            term = w_ref[k:k + 1, cs] * operand
            acc = term if acc is None else acc + term
        cv_ref[:, cs] = acc + bdw_ref[:, cs]
    c = _ln(cv_ref[...], g_ref[...], b_ref[...])
    y_ref[rows, :] = (c * jax.nn.sigmoid(c)).astype(y_ref.dtype)
    ctx_ref[0:CTX_PAD, :] = ctx_ref[STEP:STEP + CTX_PAD, :]


def _mix1(x, w_pw1, layer, state_c31, w_dw, b_dw, g, b):
    params = [w_dw, b_dw, g, b]
    param_specs = [_const_spec((CONF_W, D)), _const_spec((1, D)), _const_spec((1, D)), _const_spec((1, D))]
    out_shapes = [jax.ShapeDtypeStruct((M_TOK, D), BF16),
                  jax.ShapeDtypeStruct((1, CONF_W - 1, D), F32),
                  jax.ShapeDtypeStruct((N_STREAMS, CONF_W - 1, D), F32)]
    scratch = [pltpu.VMEM((CTX_PAD + STEP, D), F32),
               pltpu.VMEM((STEP, D), F32),
               pltpu.VMEM((SUBLANES - 1, SHIFT_ROWS, CONV_COLS), F32)]
    return _mix_block(x, w_pw1, layer, state_c31, params, param_specs, out_shapes,
                      _mix_out_specs((CONF_W - 1, D)), scratch, lambda params: None, _mix1_sub, "mix_odd",
                      separate_drain=False)


ATTN_TM_PROMPT = 512
ATTN_TM_SAMPLE = 128


def _attn_block_body(x_ref, k_ref, v_ref, g_ref, b_ref, wq_hbm, wo_hbm, *rest,
                     layer, n_tiles, streams):
    o_ref, wq_b, wo_b, stage, acc0, acc1, x_prev, q_scr, a_scr, sem = rest[-10:]
    i = pl.program_id(0)
    rows_per_stream = x_ref.shape[0] // streams

    @pl.when(i == 0)
    def _():
        acc1[...] = jnp.zeros_like(acc1)
        x_prev[...] = jnp.zeros_like(x_prev)
        _load_weight_bf16(wq_hbm.at[layer], wq_b, stage, sem)
        _load_weight_bf16(wo_hbm.at[layer], wo_b, stage, sem)

    def fill(acc_new):
        q_scr[...] = jnp.dot(x_ref[...].astype(BF16), wq_b[...], preferred_element_type=F32).astype(BF16)
        for s in range(streams):
            rows = slice(s * rows_per_stream, (s + 1) * rows_per_stream)
            for h in range(HEADS):
                hs = slice(h * HEAD_DIM, (h + 1) * HEAD_DIM)
                sc = lax.dot_general(q_scr[rows, hs], k_ref[s, :, hs].astype(BF16),
                                     (((1,), (1,)), ((), ())), preferred_element_type=F32) * ATTN_SCALE
                p = jnp.exp(sc - jnp.max(sc, axis=-1, keepdims=True))
                p = p / jnp.sum(p, axis=-1, keepdims=True)
                a_scr[rows, hs] = _bdot(p, v_ref[s, :, hs]).astype(BF16)
        acc_new[...] = jnp.dot(a_scr[...], wo_b[...], preferred_element_type=F32)

    def finish(acc_old):
        o_ref[...] = _ln(ALPHA * x_prev[...] + acc_old[...], g_ref[...], b_ref[...])
        x_prev[...] = x_ref[...]

    _two_stage_steps(i, n_tiles, acc0, acc1, fill, finish)

    @pl.when(i > n_tiles)
    def _():
        o_ref[...] = jnp.zeros_like(o_ref)


def _attn_block_call(x, k, v, g, b, w_q, w_o, layer, *, tm, streams, first_tile, n_tiles, n_clear,
                     prev_out, name):
    last = n_tiles - 1

    def in_tile(i):
        return (first_tile + jnp.minimum(i, last), 0)

    def kv_block(i):
        return (layer, jnp.minimum(i, last) if k.shape[1] > streams else 0, 0, 0)

    def out_tile(i):
        return (first_tile + jnp.maximum(i - 1, 0), 0)

    in_specs = [pl.BlockSpec((tm, D), in_tile),
                pl.BlockSpec((None, streams, N_MEM, D), kv_block),
                pl.BlockSpec((None, streams, N_MEM, D), kv_block),
                pl.BlockSpec((1, D), lambda i: (0, 0)),
                pl.BlockSpec((1, D), lambda i: (0, 0)),
                pl.BlockSpec(memory_space=pl.ANY),
                pl.BlockSpec(memory_space=pl.ANY)]
    args = [x, k, v, g, b, w_q, w_o]
    aliases = {}
    if prev_out is not None:
        in_specs.append(pl.BlockSpec(memory_space=pl.ANY))
        args.append(prev_out)
        aliases = {len(args) - 1: 0}
    return pl.pallas_call(
        functools.partial(_attn_block_body, layer=layer, n_tiles=n_tiles, streams=streams),
        grid=(n_tiles + 1 + n_clear,),
        in_specs=in_specs,
        out_specs=pl.BlockSpec((tm, D), out_tile),
        out_shape=jax.ShapeDtypeStruct((M_TOK, D), F32),
        scratch_shapes=[pltpu.VMEM((D, D), BF16),
                        pltpu.VMEM((D, D), BF16),
                        pltpu.VMEM((2, W_STAGE_ROWS, D), F32),
                        pltpu.VMEM((tm, D), F32),
                        pltpu.VMEM((tm, D), F32),
                        pltpu.VMEM((tm, D), F32),
                        pltpu.VMEM((tm, D), BF16),
                        pltpu.VMEM((tm, D), BF16),
                        pltpu.SemaphoreType.DMA((2,))],
        input_output_aliases=aliases,
        compiler_params=_cparams(1),
        name=name,
    )(*args)


def _attn_block(x, pk, pv, ck, cv, g, b, w_q, w_o, layer):
    n_sample = N_STREAMS * STEP
    out = _attn_block_call(x, pk, pv, g, b, w_q, w_o, layer, tm=ATTN_TM_PROMPT, streams=1,
                           first_tile=0, n_tiles=N_PROMPT // ATTN_TM_PROMPT,
                           n_clear=n_sample // ATTN_TM_PROMPT, prev_out=None, name="attn_prompt")
    return _attn_block_call(x, ck, cv, g, b, w_q, w_o, layer, tm=ATTN_TM_SAMPLE,
                            streams=ATTN_TM_SAMPLE // STEP, first_tile=N_PROMPT // ATTN_TM_SAMPLE,
                            n_tiles=n_sample // ATTN_TM_SAMPLE, n_clear=0, prev_out=out,
                            name="attn_sample")


TOP_K = 2
ROUTE_TM = 512
EXP_TM = 512
N_ASSIGN = M_TOK * TOP_K
N_EXP_TILES = N_ASSIGN // EXP_TM
N_VISITS = N_EXP_TILES + N_EXPERTS - 1
MOVE_TM = 256
LANE = 128
CHUNKS = D // LANE
ISSUE_UNROLL = 8


def _to_chunk_major(dst_ref, x, rows):
    for c in range(CHUNKS):
        dst_ref[pl.ds(c, rows, stride=CHUNKS), :] = x[:, c * LANE:(c + 1) * LANE]


def _from_chunk_major(src_ref, rows):
    return jnp.concatenate([src_ref[pl.ds(c, rows, stride=CHUNKS), :] for c in range(CHUNKS)], axis=-1)


def _route_body(x_ref, wr_ref, br_ref, idx_ref, wts_ref, cnt_ref, run_ref):
    i = pl.program_id(0)

    @pl.when(i == 0)
    def _():
        run_ref[...] = jnp.zeros_like(run_ref)

    logits = _bdot(x_ref[...], wr_ref[...]) + br_ref[...]
    lane = lax.broadcasted_iota(jnp.int32, logits.shape, 1)

    def first_argmax(vals):
        top = jnp.max(vals, axis=-1, keepdims=True)
        idx = jnp.min(jnp.where(vals == top, lane, ROUTE_LANES), axis=-1, keepdims=True)
        return top, idx

    g_logits = jnp.where(lane < N_GROUPS, logits, NEG)
    g_top, g_idx = first_argmax(g_logits)
    p_group = 1.0 / jnp.sum(jnp.exp(g_logits - g_top), axis=-1, keepdims=True)
    lo_lane = N_GROUPS + E_PER * g_idx
    e_logits = jnp.where(jnp.logical_and(lane >= lo_lane, lane < lo_lane + E_PER), logits, NEG)
    v1, i1 = first_argmax(e_logits)
    v2, i2 = first_argmax(jnp.where(lane == i1, NEG, e_logits))
    t = jnp.exp(v2 - v1)
    w1 = p_group / (1.0 + t)
    w2 = p_group * t / (1.0 + t)

    a1 = (lane == i1).astype(F32)
    a2 = (lane == i2).astype(F32)
    a = a1 + a2
    tm = logits.shape[0]
    earlier = (lax.broadcasted_iota(jnp.int32, (tm, tm), 1)
               < lax.broadcasted_iota(jnp.int32, (tm, tm), 0)).astype(BF16)
    before = jnp.dot(earlier, a.astype(BF16), preferred_element_type=F32) + run_ref[...]
    r1 = jnp.sum(a1 * before, axis=-1, keepdims=True)
    r2 = jnp.sum(a2 * before, axis=-1, keepdims=True)
    run_ref[...] += jnp.sum(a, axis=0, keepdims=True)
    cnt_ref[...] = run_ref[...].astype(jnp.int32)

    e1 = (i1 - N_GROUPS).astype(F32)
    e2 = (i2 - N_GROUPS).astype(F32)
    packed = (jnp.where(lane == 0, e1, 0.0) + jnp.where(lane == 1, e2, 0.0)
              + jnp.where(lane == 2, r1, 0.0) + jnp.where(lane == 3, r2, 0.0))
    idx_ref[...] = packed.T[0:8, :].astype(jnp.int32)
    wts_ref[...] = jnp.where(lane == 0, w1, 0.0) + jnp.where(lane == 1, w2, 0.0)


def _route(x, wr, br):
    m = x.shape[0]
    tm = ROUTE_TM
    return pl.pallas_call(
        _route_body,
        grid=(m // tm,),
        in_specs=[pl.BlockSpec((tm, D), lambda i: (i, 0)),
                  pl.BlockSpec((D, ROUTE_LANES), lambda i: (0, 0)),
                  pl.BlockSpec((1, ROUTE_LANES), lambda i: (0, 0))],
        out_specs=[pl.BlockSpec((8, tm), lambda i: (0, i)),
                   pl.BlockSpec((tm, ROUTE_LANES), lambda i: (i, 0)),
                   pl.BlockSpec((1, ROUTE_LANES), lambda i: (0, 0))],
        out_shape=[jax.ShapeDtypeStruct((8, m), jnp.int32),
                   jax.ShapeDtypeStruct((m, ROUTE_LANES), F32),
                   jax.ShapeDtypeStruct((1, ROUTE_LANES), jnp.int32)],
        scratch_shapes=[pltpu.VMEM((1, ROUTE_LANES), F32)],
        compiler_params=_cparams(1),
        name="moe_route",
    )(x, wr, br)


def _plan_body(cnt_ref, idx_ref, pos_ref, vt_ref, ve_ref, vlo_ref, vhi_ref, nvis_ref):
    shift = EXP_TM.bit_length() - 1
    e1 = idx_ref[0:1, :]
    e2 = idx_ref[1:2, :]
    pos1 = idx_ref[2:3, :]
    pos2 = idx_ref[3:4, :]
    start = jnp.int32(0)
    n_vis = jnp.int32(0)
    last_e = jnp.int32(0)
    for e in range(N_EXPERTS):
        cnt = cnt_ref[N_GROUPS + e]
        end = start + cnt
        pos1 = pos1 + jnp.where(e1 == e, start, 0)
        pos2 = pos2 + jnp.where(e2 == e, start, 0)
        first_tile = lax.shift_right_logical(start, shift)
        n_tiles = jnp.where(cnt > 0, lax.shift_right_logical(end + (EXP_TM - 1), shift) - first_tile, 0)

        def add_visit(k, carry, e=e, start=start, end=end, first_tile=first_tile, n_vis=n_vis):
            tile = first_tile + k
            vt_ref[n_vis + k] = tile
            ve_ref[n_vis + k] = e
            vlo_ref[n_vis + k] = jnp.maximum(start - tile * EXP_TM, 0)
            vhi_ref[n_vis + k] = jnp.minimum(end - tile * EXP_TM, EXP_TM)
            return carry

        lax.fori_loop(0, n_tiles, add_visit, 0)
        n_vis = n_vis + n_tiles
        last_e = jnp.where(cnt > 0, e, last_e)
        start = end
    nvis_ref[0] = n_vis

    def add_idle(k, carry):
        vt_ref[k] = N_EXP_TILES - 1
        ve_ref[k] = last_e
        vlo_ref[k] = 0
        vhi_ref[k] = 0
        return carry

    lax.fori_loop(n_vis, N_VISITS, add_idle, 0)
    pos_ref[...] = jnp.concatenate([pos1, pos2, jnp.zeros((6, M_TOK), jnp.int32)], axis=0)


def _plan(cnt, idx):
    smem = pl.BlockSpec(memory_space=pltpu.SMEM)
    vmem = pl.BlockSpec(memory_space=pltpu.VMEM)
    table = jax.ShapeDtypeStruct((N_VISITS,), jnp.int32)
    return pl.pallas_call(
        _plan_body,
        in_specs=[smem, vmem],
        out_specs=[vmem, smem, smem, smem, smem, smem],
        out_shape=[jax.ShapeDtypeStruct((8, M_TOK), jnp.int32), table, table, table, table,
                   jax.ShapeDtypeStruct((1,), jnp.int32)],
        name="moe_plan",
    )(cnt, idx)


def _row_copy(src_ref, src_row, dst_ref, dst_row, sem):
    return pltpu.make_async_copy(src_ref.at[pl.ds(src_row * CHUNKS, CHUNKS)],
                                 dst_ref.at[pl.ds(dst_row * CHUNKS, CHUNKS)], sem)


def _scatter_body(pos1_ref, pos2_ref, x_ref, xs_hbm, stage, sem):
    i = pl.program_id(0)
    last = pl.num_programs(0) - 1
    slot = i % 2

    def wait_slot(s):
        for _ in range(TOP_K):
            pltpu.make_async_copy(stage.at[s], xs_hbm.at[pl.ds(0, MOVE_TM * CHUNKS)], sem.at[s]).wait()

    @pl.when(i >= 2)
    def _():
        wait_slot(slot)

    _to_chunk_major(stage.at[slot], x_ref[...], MOVE_TM)

    def issue(r, carry):
        t = i * MOVE_TM + r
        _row_copy(stage.at[slot], r, xs_hbm, pos1_ref[t], sem.at[slot]).start()
        _row_copy(stage.at[slot], r, xs_hbm, pos2_ref[t], sem.at[slot]).start()
        return carry

    lax.fori_loop(0, MOVE_TM, issue, 0, unroll=ISSUE_UNROLL)

    @pl.when(i == last)
    def _():
        wait_slot(1 - slot)
        wait_slot(slot)


def _scatter(pos1, pos2, x):
    m = x.shape[0]
    assert (m // MOVE_TM) >= 2
    grid_spec = pltpu.PrefetchScalarGridSpec(
        num_scalar_prefetch=2,
        grid=(m // MOVE_TM,),
        in_specs=[pl.BlockSpec((MOVE_TM, D), lambda i, p1, p2: (i, 0))],
        out_specs=pl.BlockSpec(memory_space=pl.ANY),
        scratch_shapes=[pltpu.VMEM((2, MOVE_TM * CHUNKS, LANE), F32),
                        pltpu.SemaphoreType.DMA((2,))])
    return pl.pallas_call(
        _scatter_body,
        grid_spec=grid_spec,
        out_shape=jax.ShapeDtypeStruct((N_ASSIGN * CHUNKS, LANE), F32),
        compiler_params=_cparams(1),
        name="moe_scatter",
    )(pos1, pos2, x)


def _experts_body(vt_ref, ve_ref, vlo_ref, vhi_ref, nvis_ref, x_ref, wg_ref, wu_ref, wd_ref,
                  o_ref, wg_b, wu_b, wd_b):
    v = pl.program_id(0)
    prev = jnp.maximum(v - 1, 0)

    @pl.when(v < nvis_ref[0])
    def _():
        @pl.when(jnp.logical_or(v == 0, ve_ref[v] != ve_ref[prev]))
        def _():
            wg_b[...] = wg_ref[...].astype(BF16)
            wu_b[...] = wu_ref[...].astype(BF16)
            wd_b[...] = wd_ref[...].astype(BF16)

        xb = _from_chunk_major(x_ref, EXP_TM).astype(BF16)
        gate = jnp.dot(xb, wg_b[...], preferred_element_type=F32)
        up = jnp.dot(xb, wu_b[...], preferred_element_type=F32)
        h = (gate * jax.nn.sigmoid(gate) * up).astype(BF16)
        y = jnp.dot(h, wd_b[...], preferred_element_type=F32)
        row = lax.broadcasted_iota(jnp.int32, (EXP_TM, 1), 0)
        mine = jnp.logical_and(row >= vlo_ref[v], row < vhi_ref[v])
        first_visit = jnp.logical_or(v == 0, vt_ref[v] != vt_ref[prev])

        @pl.when(first_visit)
        def _():
            _to_chunk_major(o_ref, jnp.where(mine, y, 0.0), EXP_TM)

        @pl.when(jnp.logical_not(first_visit))
        def _():
            _to_chunk_major(o_ref, jnp.where(mine, y, _from_chunk_major(o_ref, EXP_TM)), EXP_TM)


def _experts(vt, ve, vlo, vhi, nvis, xs, w_gate, w_up, w_down, layer):
    def tile_map(v, vt, ve, vlo, vhi, nvis):
        return (vt[v], 0)

    def weight_map(v, vt, ve, vlo, vhi, nvis):
        return (layer, ve[v], 0, 0)

    grid_spec = pltpu.PrefetchScalarGridSpec(
        num_scalar_prefetch=5,
        grid=(N_VISITS,),
        in_specs=[pl.BlockSpec((EXP_TM * CHUNKS, LANE), tile_map),
                  pl.BlockSpec((None, None, D, D_EXPERT), weight_map),
                  pl.BlockSpec((None, None, D, D_EXPERT), weight_map),
                  pl.BlockSpec((None, None, D_EXPERT, D), weight_map)],
        out_specs=pl.BlockSpec((EXP_TM * CHUNKS, LANE), tile_map),
        scratch_shapes=[pltpu.VMEM((D, D_EXPERT), BF16),
                        pltpu.VMEM((D, D_EXPERT), BF16),
                        pltpu.VMEM((D_EXPERT, D), BF16)])
    return pl.pallas_call(
        _experts_body,
        grid_spec=grid_spec,
        out_shape=jax.ShapeDtypeStruct((N_ASSIGN * CHUNKS, LANE), F32),
        compiler_params=_cparams(1),
        name="moe_experts",
    )(vt, ve, vlo, vhi, nvis, xs, w_gate, w_up, w_down)


def _combine_body(pos1_ref, pos2_ref, x_ref, wts_ref, g_ref, b_ref, ys_hbm, *rest, split):
    out_refs, (ybuf, sem) = rest[:-2], rest[-2:]
    i = pl.program_id(0)
    slot = i % 2

    def gather_rows(step, s):
        def issue(r, carry):
            t = step * MOVE_TM + r
            _row_copy(ys_hbm, pos1_ref[t], ybuf.at[s, 0], r, sem.at[s]).start()
            _row_copy(ys_hbm, pos2_ref[t], ybuf.at[s, 1], r, sem.at[s]).start()
            return carry
        lax.fori_loop(0, MOVE_TM, issue, 0, unroll=ISSUE_UNROLL)

    @pl.when(i == 0)
    def _():
        gather_rows(0, 0)

    @pl.when(i + 1 < pl.num_programs(0))
    def _():
        gather_rows(i + 1, 1 - slot)

    for k in range(TOP_K):
        pltpu.make_async_copy(ys_hbm.at[pl.ds(0, MOVE_TM * CHUNKS)], ybuf.at[slot, k], sem.at[slot]).wait()
    y = (wts_ref[:, 0:1] * _from_chunk_major(ybuf.at[slot, 0], MOVE_TM)
         + wts_ref[:, 1:2] * _from_chunk_major(ybuf.at[slot, 1], MOVE_TM))
    out = _ln(ALPHA * x_ref[...] + y, g_ref[...], b_ref[...])
    if split is None:
        out_refs[0][...] = out
    else:
        @pl.when(i < split)
        def _():
            out_refs[0][...] = out

        @pl.when(i >= split)
        def _():
            out_refs[1][...] = out


def _combine(pos1, pos2, x, wts, g, b, ys, split_rows=None):
    m = x.shape[0]
    if split_rows is None:
        split = None
        out_specs = pl.BlockSpec((MOVE_TM, D), lambda i, p1, p2: (i, 0))
        out_shape = jax.ShapeDtypeStruct((m, D), F32)
    else:
        split = split_rows // MOVE_TM
        out_specs = [pl.BlockSpec((MOVE_TM, D), lambda i, p1, p2: (jnp.minimum(i, split - 1), 0)),
                     pl.BlockSpec((MOVE_TM, D), lambda i, p1, p2: (jnp.maximum(i - split, 0), 0))]
        out_shape = [jax.ShapeDtypeStruct((split_rows, D), F32),
                     jax.ShapeDtypeStruct((m - split_rows, D), F32)]
    grid_spec = pltpu.PrefetchScalarGridSpec(
        num_scalar_prefetch=2,
        grid=(m // MOVE_TM,),
        in_specs=[pl.BlockSpec((MOVE_TM, D), lambda i, p1, p2: (i, 0)),
                  pl.BlockSpec((MOVE_TM, ROUTE_LANES), lambda i, p1, p2: (i, 0)),
                  pl.BlockSpec((1, D), lambda i, p1, p2: (0, 0)),
                  pl.BlockSpec((1, D), lambda i, p1, p2: (0, 0)),
                  pl.BlockSpec(memory_space=pl.ANY)],
        out_specs=out_specs,
        scratch_shapes=[pltpu.VMEM((2, TOP_K, MOVE_TM * CHUNKS, LANE), F32),
                        pltpu.SemaphoreType.DMA((2,))])
    return pl.pallas_call(
        functools.partial(_combine_body, split=split),
        grid_spec=grid_spec,
        out_shape=out_shape,
        compiler_params=_cparams(1),
        name="moe_combine",
    )(pos1, pos2, x, wts, g, b, ys)


def _moe(x, wr, br, w_gate, w_up, w_down, layer, g, b, split_rows=None):
    idx, wts, cnt = _route(x, wr, br)
    pos, vt, ve, vlo, vhi, nvis = _plan(cnt.reshape(ROUTE_LANES), idx)
    xs = _scatter(pos[0], pos[1], x)
    ys = _experts(vt, ve, vlo, vhi, nvis, xs, w_gate, w_up, w_down, layer)
    return _combine(pos[0], pos[1], x, wts, g, b, ys, split_rows)


def _router_params(w_rg, b_rg, w_re, b_re):
    w = jnp.concatenate([w_rg, jnp.transpose(w_re, (1, 0, 2)).reshape(D, N_EXPERTS)], axis=1)
    bias = jnp.concatenate([b_rg, b_re.reshape(N_EXPERTS)])
    pad = ROUTE_LANES - w.shape[1]
    return jnp.pad(w, ((0, 0), (0, pad))), jnp.pad(bias, (0, pad))[None, :]


def kernel(x_prompt, x_sample, mem_prompt, cache_mem_k, cache_mem_v, state_conv3, state_conv31,
           ln_gain, ln_bias, w_in_ab, w_conv3, ln_sgu_gain, ln_sgu_bias, w_spatial, b_spatial, w_out_ab,
           w_pw1, w_dw31, b_dw31, ln_conf_gain, ln_conf_bias, w_pw2,
           w_mem_q, w_mem_k, w_mem_v, w_mem_o,
           w_route_group, b_route_group, w_route_expert, b_route_expert,
           w_exp_gate, w_exp_up, w_exp_down):
    depth = ln_gain.shape[0]
    x = (x_prompt.reshape(N_PROMPT, D), x_sample.reshape(N_STREAMS * STEP, D))

    mem = mem_prompt.reshape(N_MEM, D)
    pk, pk_b = _mem_kv(mem, w_mem_k, "mem_k")
    pv, pv_b = _mem_kv(mem, w_mem_v, "mem_v")
    pk_b = pk_b.reshape(depth, 1, N_MEM, D)
    pv_b = pv_b.reshape(depth, 1, N_MEM, D)
    ck = cache_mem_k.reshape(depth, N_STREAMS, N_MEM, D)
    cv = cache_mem_v.reshape(depth, N_STREAMS, N_MEM, D)

    def ln_params(l, k):
        return ln_gain[l, k][None, :], ln_bias[l, k][None, :]

    c3_prompt, c3_sample, c31_prompt, c31_sample, v_rows = [], [], [], [], []
    for l in range(depth):
        i = l // 2
        if l % 2 == 0:
            y_mix, c3p, c3s, v = _mix0(
                x, w_in_ab, i, state_conv3[i], w_conv3[i], ln_sgu_gain[i][None, :], ln_sgu_bias[i][None, :],
                w_spatial[i], b_spatial[i][:, :, None])
            c3_prompt.append(c3p)
            c3_sample.append(c3s)
            v_rows.append(v)
            w_out = w_out_ab
        else:
            y_mix, c31p, c31s = _mix1(
                x, w_pw1, i, state_conv31[i], w_dw31[i], b_dw31[i][None, :],
                ln_conf_gain[i][None, :], ln_conf_bias[i][None, :])
            c31_prompt.append(c31p)
            c31_sample.append(c31s)
            w_out = w_pw2
        x = _mm_ln(y_mix, w_out, i, x, *ln_params(l, 0), name="mix_out_ln")
        x = _attn_block(x, pk_b, pv_b, ck, cv, *ln_params(l, 1), w_mem_q, w_mem_o, l)
        wr, br = _router_params(w_route_group[l], b_route_group[l], w_route_expert[l], b_route_expert[l])
        x = _moe(x, wr, br, w_exp_gate, w_exp_up, w_exp_down, l, *ln_params(l, 2),
                 split_rows=N_PROMPT if l == depth - 1 else None)

    y_prompt = x[0].reshape(1, N_PROMPT, D)
    y_sample = x[1].reshape(N_STREAMS, STEP, D)
    prompt_mem_k = pk.reshape(depth, 1, N_MEM, HEADS, HEAD_DIM)
    prompt_mem_v = pv.reshape(depth, 1, N_MEM, HEADS, HEAD_DIM)
    prompt_conv3 = jnp.stack(c3_prompt)
    sample_conv3 = jnp.stack(c3_sample)
    prompt_conv31 = jnp.stack(c31_prompt)
    sample_conv31 = jnp.stack(c31_sample)
    sample_sgu_v = jnp.stack(v_rows)
    return (y_prompt, y_sample, prompt_mem_k, prompt_mem_v, prompt_conv3, prompt_conv31,
            sample_conv3, sample_conv31, sample_sgu_v)
```

```python
import functools

import jax
import jax.numpy as jnp
from jax import lax
from jax.experimental import pallas as pl
from jax.experimental.pallas import tpu as pltpu

F32 = jnp.float32
BF16 = jnp.bfloat16

D = 2048
N_PROMPT = 8192
N_STREAMS = 16
STEP = 64
NP_STEPS = N_PROMPT // STEP
M_TOK = N_PROMPT + N_STREAMS * STEP
N_MEM = 256
HEADS = 4
HEAD_DIM = D // HEADS
D_A = 1024
D_B = 1024
SGU_HEADS = 8
N_GROUPS = 4
E_PER = 4
N_EXPERTS = 16
D_EXPERT = 512
LN_EPS = 1e-5
ALPHA = (2.0 * 2) ** 0.25
ATTN_SCALE = HEAD_DIM ** -0.5
ROUTE_LANES = 128
NEG = -1e30

VMEM_LIMIT_BYTES = 60000 * 1024


def _cparams(n_axes):
    return pltpu.CompilerParams(
        dimension_semantics=("arbitrary",) * n_axes,
        vmem_limit_bytes=VMEM_LIMIT_BYTES)


def _ln(x, g, b):
    mu = jnp.mean(x, axis=-1, keepdims=True)
    xc = x - mu
    var = jnp.mean(xc * xc, axis=-1, keepdims=True)
    return xc * lax.rsqrt(var + LN_EPS) * g + b


def _bdot(a, b):
    return jnp.dot(a.astype(BF16), b.astype(BF16), preferred_element_type=F32)


def _conv_operand(x):
    return x.astype(BF16).astype(F32)


def _mem_kv_body(x_ref, w_ref, o_ref, ob_ref):
    y = _bdot(x_ref[...], w_ref[...])
    o_ref[...] = y
    ob_ref[...] = y.astype(BF16)


def _mem_kv(mem, w, name):
    depth, k, n = w.shape
    tn = 512
    out_block = pl.BlockSpec((None, N_MEM, tn), lambda l, j: (l, 0, j))
    return pl.pallas_call(
        _mem_kv_body,
        grid=(depth, n // tn),
        in_specs=[pl.BlockSpec((N_MEM, k), lambda l, j: (0, 0)),
                  pl.BlockSpec((None, k, tn), lambda l, j: (l, 0, j))],
        out_specs=[out_block, out_block],
        out_shape=[jax.ShapeDtypeStruct((depth, N_MEM, n), F32),
                   jax.ShapeDtypeStruct((depth, N_MEM, n), BF16)],
        compiler_params=_cparams(2),
        name=name,
    )(mem, w)


MM_LN_TM = 512
W_STAGE_ROWS = 128


def _load_weight_bf16(w_hbm, wb, stage, sem):
    rows = stage.shape[1]
    n_chunks = wb.shape[0] // rows

    def copy(k):
        return pltpu.make_async_copy(w_hbm.at[pl.ds(k * rows, rows)], stage.at[k % 2], sem.at[k % 2])

    copy(0).start()
    for k in range(n_chunks):
        if k + 1 < n_chunks:
            copy(k + 1).start()
        copy(k).wait()
        wb[k * rows:(k + 1) * rows, :] = stage[k % 2].astype(BF16)


def _two_stage_steps(i, n_tiles, buf0, buf1, fill, finish, separate_drain=True):
    last_filling = n_tiles if separate_drain else n_tiles + 1

    @pl.when(jnp.logical_and(i < last_filling, i % 2 == 0))
    def _():
        fill(buf0)
        finish(buf1)

    @pl.when(jnp.logical_and(i < last_filling, i % 2 == 1))
    def _():
        fill(buf1)
        finish(buf0)

    if separate_drain:
        @pl.when(i == n_tiles)
        def _():
            finish(buf1 if n_tiles % 2 == 0 else buf0)


def _mm_ln_body(x_ref, *rest, layer, res_split, n_tiles):
    if res_split is None:
        r_ref, g_ref, b_ref, w_hbm, o_ref, wb, stage, acc0, acc1, sem = rest
    else:
        ra_ref, rb_ref, g_ref, b_ref, w_hbm, o_ref, wb, stage, acc0, acc1, sem = rest
    i = pl.program_id(0)

    @pl.when(i == 0)
    def _():
        acc1[...] = jnp.zeros_like(acc1)
        _load_weight_bf16(w_hbm.at[layer], wb, stage, sem)

    def fill(acc):
        acc[...] = jnp.dot(x_ref[...], wb[...], preferred_element_type=F32)

    def finish(acc):
        if res_split is None:
            res = r_ref[...]
        else:
            res = jnp.where(i - 1 < res_split, ra_ref[...], rb_ref[...])
        o_ref[...] = _ln(ALPHA * res + acc[...], g_ref[...], b_ref[...])

    _two_stage_steps(i, n_tiles, acc0, acc1, fill, finish)


def _mm_ln(x, w, layer, res, g, b, name):
    m, k = x.shape
    n = w.shape[-1]
    tm = MM_LN_TM
    n_tiles = m // tm
    assert x.dtype == BF16

    def done_tile(i):
        return (jnp.maximum(i - 1, 0), 0)

    if isinstance(res, tuple):
        res_split = res[0].shape[0] // tm
        res_specs = [pl.BlockSpec((tm, n), lambda i: (jnp.clip(i - 1, 0, res_split - 1), 0)),
                     pl.BlockSpec((tm, n), lambda i: (jnp.maximum(i - 1 - res_split, 0), 0))]
    else:
        res_split = None
        res_specs = [pl.BlockSpec((tm, n), done_tile)]
        res = (res,)

    return pl.pallas_call(
        functools.partial(_mm_ln_body, layer=layer, res_split=res_split, n_tiles=n_tiles),
        grid=(n_tiles + 1,),
        in_specs=[pl.BlockSpec((tm, k), lambda i: (jnp.minimum(i, n_tiles - 1), 0))] + res_specs + [
                  pl.BlockSpec((1, n), lambda i: (0, 0)),
                  pl.BlockSpec((1, n), lambda i: (0, 0)),
                  pl.BlockSpec(memory_space=pl.ANY)],
        out_specs=pl.BlockSpec((tm, n), done_tile),
        out_shape=jax.ShapeDtypeStruct((m, n), F32),
        scratch_shapes=[pltpu.VMEM((k, n), BF16),
                        pltpu.VMEM((2, W_STAGE_ROWS, n), F32),
                        pltpu.VMEM((tm, n), F32),
                        pltpu.VMEM((tm, n), F32),
                        pltpu.SemaphoreType.DMA((2,))],
        compiler_params=_cparams(1),
        name=name,
    )(x, *res, g, b, w)


MIX_TM = 256
MIX_SUB = MIX_TM // STEP
N_PROMPT_TILES = N_PROMPT // MIX_TM
N_MIX_TILES = M_TOK // MIX_TM
N_SAMPLE_TILES = N_MIX_TILES - N_PROMPT_TILES


def _mix_block_body(*refs, n_x, n_params, n_outs, layer, prepare, mixer, separate_drain):
    x_refs, refs = refs[:n_x], refs[n_x:]
    st_ref, refs = refs[0], refs[1:]
    params, refs = refs[:n_params], refs[n_params:]
    w_hbm, refs = refs[0], refs[1:]
    outs, refs = refs[:n_outs], refs[n_outs:]
    wb, stage, p0, p1, sem = refs[:5]
    scratch = refs[5:]
    i = pl.program_id(0)

    @pl.when(i == 0)
    def _():
        p1[...] = jnp.zeros_like(p1)
        for ref in tuple(scratch) + tuple(outs[1:2]):
            ref[...] = jnp.zeros_like(ref)
        _load_weight_bf16(w_hbm.at[layer], wb, stage, sem)

    def fill(p_new):
        if n_x == 1:
            x = x_refs[0][...]
        else:
            x = jnp.where(i < N_PROMPT_TILES, x_refs[0][...], x_refs[1][...])
        p_new[...] = jnp.dot(x.astype(BF16), wb[...], preferred_element_type=F32)

    def finish(p_old):
        is_sample = i - 1 >= N_PROMPT_TILES
        pre = prepare(params)
        for j in range(MIX_SUB):
            mixer(p_old, j, is_sample, st_ref, params, outs, scratch, pre)

    _two_stage_steps(i, N_MIX_TILES, p0, p1, fill, finish, separate_drain)


def _mixed_tile(i):
    return jnp.maximum(i - 1, 0)


def _sample_tile(i):
    return jnp.clip(i - 1 - N_PROMPT_TILES, 0, N_SAMPLE_TILES - 1)


def _mix_block(x, w, layer, state, params, param_specs, out_shapes, out_specs, scratch_shapes,
               prepare, mixer, name, separate_drain=True):
    k, n = w.shape[-2:]
    last = N_MIX_TILES - 1
    if isinstance(x, tuple):
        x_specs = [pl.BlockSpec((MIX_TM, k), lambda i: (jnp.minimum(i, N_PROMPT_TILES - 1), 0)),
                   pl.BlockSpec((MIX_TM, k), lambda i: (jnp.clip(i - N_PROMPT_TILES, 0, N_SAMPLE_TILES - 1), 0))]
    else:
        x_specs = [pl.BlockSpec((MIX_TM, k), lambda i: (jnp.minimum(i, last), 0))]
        x = (x,)
    state_spec = pl.BlockSpec((MIX_SUB,) + state.shape[1:], lambda i: (_sample_tile(i), 0, 0))
    return pl.pallas_call(
        functools.partial(_mix_block_body, n_x=len(x), n_params=len(params), n_outs=len(out_shapes),
                          layer=layer, prepare=prepare, mixer=mixer, separate_drain=separate_drain),
        grid=(N_MIX_TILES + 1,),
        in_specs=x_specs + [state_spec] + param_specs + [pl.BlockSpec(memory_space=pl.ANY)],
        out_specs=out_specs,
        out_shape=out_shapes,
        scratch_shapes=[pltpu.VMEM((k, n), BF16),
                        pltpu.VMEM((2, W_STAGE_ROWS, n), F32),
                        pltpu.VMEM((MIX_TM, n), F32),
                        pltpu.VMEM((MIX_TM, n), F32),
                        pltpu.SemaphoreType.DMA((2,))] + scratch_shapes,
        compiler_params=_cparams(1),
        name=name,
    )(*x, state, *params, w)


def _mix_out_specs(state_shape):
    return [pl.BlockSpec((MIX_TM, D), lambda i: (_mixed_tile(i), 0)),
            pl.BlockSpec((None,) + state_shape, lambda i: (0, 0, 0)),
            pl.BlockSpec((MIX_SUB,) + state_shape, lambda i: (_sample_tile(i), 0, 0))]


def _const_spec(shape):
    return pl.BlockSpec(shape, lambda i: (0,) * len(shape))


def _mix0_prepare(params):
    ws_ref = params[3]
    row = lax.broadcasted_iota(jnp.int32, (2 * STEP, 2 * STEP), 0)
    col = lax.broadcasted_iota(jnp.int32, (2 * STEP, 2 * STEP), 1)
    mask = (col // STEP) <= (row // STEP)
    return [jnp.where(mask, ws_ref[h], 0.0).astype(BF16) for h in range(SGU_HEADS)]


def _mix0_sub(p_ref, j, is_sample, st_ref, params, outs, scratch, w_masked):
    wc_ref, g_ref, b_ref, _, bs_ref = params
    y_ref, c3p_ref, c3s_ref, vrow_ref = outs
    ctx_ref, vprev_ref = scratch
    rows = slice(j * STEP, (j + 1) * STEP)

    z = p_ref[rows, 2 * D_A:3 * D_A] * p_ref[rows, 0:D_A]
    ctx_ref[6:8, :] = jnp.where(is_sample, _conv_operand(st_ref[j]), ctx_ref[6:8, :])
    zc = _conv_operand(z)
    ctx_ref[8:8 + STEP, :] = zc
    conv = (wc_ref[0:1, :] * ctx_ref[6:6 + STEP, :]
            + wc_ref[1:2, :] * ctx_ref[7:7 + STEP, :]
            + wc_ref[2:3, :] * zc)
    y_ref[rows, 0:D_A] = (p_ref[rows, D_A:2 * D_A] * conv).astype(y_ref.dtype)
    tail = z[STEP - 2:STEP, :]
    c3s_ref[j] = tail
    if j == MIX_SUB - 1:
        c3p_ref[...] = jnp.where(is_sample, c3p_ref[...], tail)
    ctx_ref[0:8, :] = ctx_ref[STEP:STEP + 8, :]

    u = jax.nn.gelu(p_ref[rows, 3 * D_A:3 * D_A + D_B])
    v = _ln(jax.nn.gelu(p_ref[rows, 3 * D_A + D_B:3 * D_A + 2 * D_B]), g_ref[...], b_ref[...])
    vrow_ref[j] = v
    vb = v.astype(BF16)
    for h in range(SGU_HEADS):
        hs = slice(h * 128, (h + 1) * 128)
        wm = w_masked[h]
        v_h = vb[:, hs]
        f = jnp.dot(wm[0:STEP, 0:STEP], v_h, preferred_element_type=F32) + bs_ref[h, 0:STEP, :]
        if j % 2 == 1:
            f_second = (jnp.dot(wm[STEP:, 0:STEP], vprev_ref[:, hs], preferred_element_type=F32)
                        + jnp.dot(wm[STEP:, STEP:], v_h, preferred_element_type=F32)
                        + bs_ref[h, STEP:, :])
            f = jnp.where(is_sample, f, f_second)
        y_ref[rows, D_A + h * 128:D_A + (h + 1) * 128] = (u[:, hs] * f).astype(y_ref.dtype)
    vprev_ref[...] = vb


def _mix0(x, w_in, layer, state_c3, w_conv3, g, b, w_spatial, b_spatial_col):
    params = [w_conv3, g, b, w_spatial, b_spatial_col]
    param_specs = [_const_spec((3, D_A)), _const_spec((1, D_B)), _const_spec((1, D_B)),
                   _const_spec((SGU_HEADS, 128, 128)), _const_spec((SGU_HEADS, 128, 1))]
    out_shapes = [jax.ShapeDtypeStruct((M_TOK, D), BF16),
                  jax.ShapeDtypeStruct((1, 2, D_A), F32),
                  jax.ShapeDtypeStruct((N_STREAMS, 2, D_A), F32),
                  jax.ShapeDtypeStruct((N_STREAMS, STEP, D_B), F32)]
    out_specs = _mix_out_specs((2, D_A)) + [
        pl.BlockSpec((MIX_SUB, STEP, D_B), lambda i: (_sample_tile(i), 0, 0))]
    scratch = [pltpu.VMEM((STEP + 8, D_A), F32), pltpu.VMEM((STEP, D_B), BF16)]
    return _mix_block(x, w_in, layer, state_c3, params, param_specs, out_shapes, out_specs, scratch,
                      _mix0_prepare, _mix0_sub, "mix_even")


CONF_W = 31
CTX_PAD = 32
CONV_COLS = 256
SUBLANES = 8
SHIFT_ROWS = CTX_PAD + STEP - SUBLANES


def _mix1_sub(p_ref, j, is_sample, st_ref, params, outs, scratch, _):
    w_ref, bdw_ref, g_ref, b_ref = params
    y_ref, c31p_ref, c31s_ref = outs
    ctx_ref, cv_ref, sh_ref = scratch
    rows = slice(j * STEP, (j + 1) * STEP)
    first = CTX_PAD - (CONF_W - 1)

    glu = p_ref[rows, 0:D] * jax.nn.sigmoid(p_ref[rows, D:2 * D])
    ctx_ref[first:CTX_PAD, :] = jnp.where(is_sample, _conv_operand(st_ref[j]), ctx_ref[first:CTX_PAD, :])
    ctx_ref[CTX_PAD:CTX_PAD + STEP, :] = _conv_operand(glu)
    tail = glu[STEP - (CONF_W - 1):STEP, :]
    c31s_ref[j] = tail
    if j == MIX_SUB - 1:
        c31p_ref[...] = jnp.where(is_sample, c31p_ref[...], tail)
    for c in range(D // CONV_COLS):
        cs = slice(c * CONV_COLS, (c + 1) * CONV_COLS)
        for s in range(1, SUBLANES):
            sh_ref[s - 1, :, :] = ctx_ref[s:s + SHIFT_ROWS, cs]
        acc = None
        for k in range(CONF_W):
            q, s = divmod(first + k, SUBLANES)
            taps = slice(q * SUBLANES, q * SUBLANES + STEP)
            operand = ctx_ref[taps, cs] if s == 0 else sh_ref[s - 1, taps, :]
            term = operand.reshape(STEP // SUBLANES, SUBLANES, CONV_COLS) * w_ref[k, :, cs][None]
            acc = term if acc is None else acc + term
        cv_ref[:, cs] = acc.reshape(STEP, CONV_COLS) + bdw_ref[:, cs]
    c = _ln(cv_ref[...], g_ref[...], b_ref[...])
    y_ref[rows, :] = (c * jax.nn.sigmoid(c)).astype(y_ref.dtype)
    ctx_ref[0:CTX_PAD, :] = ctx_ref[STEP:STEP + CTX_PAD, :]


def _mix1(x, w_pw1, layer, state_c31, w_dw, b_dw, g, b):
    w_taps = jnp.broadcast_to(w_dw[:, None, :], (CONF_W, SUBLANES, D))
    params = [w_taps, b_dw, g, b]
    param_specs = [_const_spec((CONF_W, SUBLANES, D)), _const_spec((1, D)), _const_spec((1, D)), _const_spec((1, D))]
    out_shapes = [jax.ShapeDtypeStruct((M_TOK, D), BF16),
                  jax.ShapeDtypeStruct((1, CONF_W - 1, D), F32),
                  jax.ShapeDtypeStruct((N_STREAMS, CONF_W - 1, D), F32)]
    scratch = [pltpu.VMEM((CTX_PAD + STEP, D), F32),
               pltpu.VMEM((STEP, D), F32),
               pltpu.VMEM((SUBLANES - 1, SHIFT_ROWS, CONV_COLS), F32)]
    return _mix_block(x, w_pw1, layer, state_c31, params, param_specs, out_shapes,
                      _mix_out_specs((CONF_W - 1, D)), scratch, lambda params: None, _mix1_sub, "mix_odd",
                      separate_drain=False)


ATTN_TM_PROMPT = 512
ATTN_TM_SAMPLE = 128


def _attn_block_body(x_ref, k_ref, v_ref, g_ref, b_ref, wq_hbm, wo_hbm, *rest,
                     layer, n_tiles, streams):
    o_ref, wq_b, wo_b, stage, acc0, acc1, x_prev, q_scr, a_scr, sem = rest[-10:]
    i = pl.program_id(0)
    rows_per_stream = x_ref.shape[0] // streams

    @pl.when(i == 0)
    def _():
        acc1[...] = jnp.zeros_like(acc1)
        x_prev[...] = jnp.zeros_like(x_prev)
        _load_weight_bf16(wq_hbm.at[layer], wq_b, stage, sem)
        _load_weight_bf16(wo_hbm.at[layer], wo_b, stage, sem)

    def fill(acc_new):
        q_scr[...] = jnp.dot(x_ref[...].astype(BF16), wq_b[...], preferred_element_type=F32).astype(BF16)
        for s in range(streams):
            rows = slice(s * rows_per_stream, (s + 1) * rows_per_stream)
            for h in range(HEADS):
                hs = slice(h * HEAD_DIM, (h + 1) * HEAD_DIM)
                sc = lax.dot_general(q_scr[rows, hs], k_ref[s, :, hs].astype(BF16),
                                     (((1,), (1,)), ((), ())), preferred_element_type=F32) * ATTN_SCALE
                p = jnp.exp(sc - jnp.max(sc, axis=-1, keepdims=True))
                p = p / jnp.sum(p, axis=-1, keepdims=True)
                a_scr[rows, hs] = _bdot(p, v_ref[s, :, hs]).astype(BF16)
        acc_new[...] = jnp.dot(a_scr[...], wo_b[...], preferred_element_type=F32)

    def finish(acc_old):
        o_ref[...] = _ln(ALPHA * x_prev[...] + acc_old[...], g_ref[...], b_ref[...])
        x_prev[...] = x_ref[...]

    _two_stage_steps(i, n_tiles, acc0, acc1, fill, finish)

    @pl.when(i > n_tiles)
    def _():
        o_ref[...] = jnp.zeros_like(o_ref)


def _attn_block_call(x, k, v, g, b, w_q, w_o, layer, *, tm, streams, first_tile, n_tiles, n_clear,
                     prev_out, name):
    last = n_tiles - 1

    def in_tile(i):
        return (first_tile + jnp.minimum(i, last), 0)

    def kv_block(i):
        return (layer, jnp.minimum(i, last) if k.shape[1] > streams else 0, 0, 0)

    def out_tile(i):
        return (first_tile + jnp.maximum(i - 1, 0), 0)

    in_specs = [pl.BlockSpec((tm, D), in_tile),
                pl.BlockSpec((None, streams, N_MEM, D), kv_block),
                pl.BlockSpec((None, streams, N_MEM, D), kv_block),
                pl.BlockSpec((1, D), lambda i: (0, 0)),
                pl.BlockSpec((1, D), lambda i: (0, 0)),
                pl.BlockSpec(memory_space=pl.ANY),
                pl.BlockSpec(memory_space=pl.ANY)]
    args = [x, k, v, g, b, w_q, w_o]
    aliases = {}
    if prev_out is not None:
        in_specs.append(pl.BlockSpec(memory_space=pl.ANY))
        args.append(prev_out)
        aliases = {len(args) - 1: 0}
    return pl.pallas_call(
        functools.partial(_attn_block_body, layer=layer, n_tiles=n_tiles, streams=streams),
        grid=(n_tiles + 1 + n_clear,),
        in_specs=in_specs,
        out_specs=pl.BlockSpec((tm, D), out_tile),
        out_shape=jax.ShapeDtypeStruct((M_TOK, D), F32),
        scratch_shapes=[pltpu.VMEM((D, D), BF16),
                        pltpu.VMEM((D, D), BF16),
                        pltpu.VMEM((2, W_STAGE_ROWS, D), F32),
                        pltpu.VMEM((tm, D), F32),
                        pltpu.VMEM((tm, D), F32),
                        pltpu.VMEM((tm, D), F32),
                        pltpu.VMEM((tm, D), BF16),
                        pltpu.VMEM((tm, D), BF16),
                        pltpu.SemaphoreType.DMA((2,))],
        input_output_aliases=aliases,
        compiler_params=_cparams(1),
        name=name,
    )(*args)


def _attn_block(x, pk, pv, ck, cv, g, b, w_q, w_o, layer):
    n_sample = N_STREAMS * STEP
    out = _attn_block_call(x, pk, pv, g, b, w_q, w_o, layer, tm=ATTN_TM_PROMPT, streams=1,
                           first_tile=0, n_tiles=N_PROMPT // ATTN_TM_PROMPT,
                           n_clear=n_sample // ATTN_TM_PROMPT, prev_out=None, name="attn_prompt")
    return _attn_block_call(x, ck, cv, g, b, w_q, w_o, layer, tm=ATTN_TM_SAMPLE,
                            streams=ATTN_TM_SAMPLE // STEP, first_tile=N_PROMPT // ATTN_TM_SAMPLE,
                            n_tiles=n_sample // ATTN_TM_SAMPLE, n_clear=0, prev_out=out,
                            name="attn_sample")


TOP_K = 2
ROUTE_TM = 512
EXP_TM = 512
N_ASSIGN = M_TOK * TOP_K
N_EXP_TILES = N_ASSIGN // EXP_TM
N_VISITS = N_EXP_TILES + N_EXPERTS - 1
MOVE_TM = 256
LANE = 128
CHUNKS = D // LANE
ISSUE_UNROLL = 8


def _to_chunk_major(dst_ref, x, rows):
    for c in range(CHUNKS):
        dst_ref[pl.ds(c, rows, stride=CHUNKS), :] = x[:, c * LANE:(c + 1) * LANE]


def _from_chunk_major(src_ref, rows):
    return jnp.concatenate([src_ref[pl.ds(c, rows, stride=CHUNKS), :] for c in range(CHUNKS)], axis=-1)


def _route_body(x_ref, wr_ref, br_ref, idx_ref, wts_ref, cnt_ref, run_ref):
    i = pl.program_id(0)

    @pl.when(i == 0)
    def _():
        run_ref[...] = jnp.zeros_like(run_ref)

    logits = _bdot(x_ref[...], wr_ref[...]) + br_ref[...]
    lane = lax.broadcasted_iota(jnp.int32, logits.shape, 1)

    def first_argmax(vals):
        top = jnp.max(vals, axis=-1, keepdims=True)
        idx = jnp.min(jnp.where(vals == top, lane, ROUTE_LANES), axis=-1, keepdims=True)
        return top, idx

    g_logits = jnp.where(lane < N_GROUPS, logits, NEG)
    g_top, g_idx = first_argmax(g_logits)
    p_group = 1.0 / jnp.sum(jnp.exp(g_logits - g_top), axis=-1, keepdims=True)
    lo_lane = N_GROUPS + E_PER * g_idx
    e_logits = jnp.where(jnp.logical_and(lane >= lo_lane, lane < lo_lane + E_PER), logits, NEG)
    v1, i1 = first_argmax(e_logits)
    v2, i2 = first_argmax(jnp.where(lane == i1, NEG, e_logits))
    t = jnp.exp(v2 - v1)
    w1 = p_group / (1.0 + t)
    w2 = p_group * t / (1.0 + t)

    a1 = (lane == i1).astype(F32)
    a2 = (lane == i2).astype(F32)
    a = a1 + a2
    tm = logits.shape[0]
    earlier = (lax.broadcasted_iota(jnp.int32, (tm, tm), 1)
               < lax.broadcasted_iota(jnp.int32, (tm, tm), 0)).astype(BF16)
    before = jnp.dot(earlier, a.astype(BF16), preferred_element_type=F32) + run_ref[...]
    r1 = jnp.sum(a1 * before, axis=-1, keepdims=True)
    r2 = jnp.sum(a2 * before, axis=-1, keepdims=True)
    run_ref[...] += jnp.sum(a, axis=0, keepdims=True)
    cnt_ref[...] = run_ref[...].astype(jnp.int32)

    e1 = (i1 - N_GROUPS).astype(F32)
    e2 = (i2 - N_GROUPS).astype(F32)
    packed = (jnp.where(lane == 0, e1, 0.0) + jnp.where(lane == 1, e2, 0.0)
              + jnp.where(lane == 2, r1, 0.0) + jnp.where(lane == 3, r2, 0.0))
    idx_ref[...] = packed.T[0:8, :].astype(jnp.int32)
    wts_ref[...] = jnp.where(lane == 0, w1, 0.0) + jnp.where(lane == 1, w2, 0.0)


def _route(x, wr, br):
    m = x.shape[0]
    tm = ROUTE_TM
    return pl.pallas_call(
        _route_body,
        grid=(m // tm,),
        in_specs=[pl.BlockSpec((tm, D), lambda i: (i, 0)),
                  pl.BlockSpec((D, ROUTE_LANES), lambda i: (0, 0)),
                  pl.BlockSpec((1, ROUTE_LANES), lambda i: (0, 0))],
        out_specs=[pl.BlockSpec((8, tm), lambda i: (0, i)),
                   pl.BlockSpec((tm, ROUTE_LANES), lambda i: (i, 0)),
                   pl.BlockSpec((1, ROUTE_LANES), lambda i: (0, 0))],
        out_shape=[jax.ShapeDtypeStruct((8, m), jnp.int32),
                   jax.ShapeDtypeStruct((m, ROUTE_LANES), F32),
                   jax.ShapeDtypeStruct((1, ROUTE_LANES), jnp.int32)],
        scratch_shapes=[pltpu.VMEM((1, ROUTE_LANES), F32)],
        compiler_params=_cparams(1),
        name="moe_route",
    )(x, wr, br)


def _plan_body(cnt_ref, idx_ref, pos_ref, vt_ref, ve_ref, vlo_ref, vhi_ref, nvis_ref):
    shift = EXP_TM.bit_length() - 1
    e1 = idx_ref[0:1, :]
    e2 = idx_ref[1:2, :]
    pos1 = idx_ref[2:3, :]
    pos2 = idx_ref[3:4, :]
    start = jnp.int32(0)
    n_vis = jnp.int32(0)
    last_e = jnp.int32(0)
    for e in range(N_EXPERTS):
        cnt = cnt_ref[N_GROUPS + e]
        end = start + cnt
        pos1 = pos1 + jnp.where(e1 == e, start, 0)
        pos2 = pos2 + jnp.where(e2 == e, start, 0)
        first_tile = lax.shift_right_logical(start, shift)
        n_tiles = jnp.where(cnt > 0, lax.shift_right_logical(end + (EXP_TM - 1), shift) - first_tile, 0)

        def add_visit(k, carry, e=e, start=start, end=end, first_tile=first_tile, n_vis=n_vis):
            tile = first_tile + k
            vt_ref[n_vis + k] = tile
            ve_ref[n_vis + k] = e
            vlo_ref[n_vis + k] = jnp.maximum(start - tile * EXP_TM, 0)
            vhi_ref[n_vis + k] = jnp.minimum(end - tile * EXP_TM, EXP_TM)
            return carry

        lax.fori_loop(0, n_tiles, add_visit, 0)
        n_vis = n_vis + n_tiles
        last_e = jnp.where(cnt > 0, e, last_e)
        start = end
    nvis_ref[0] = n_vis

    def add_idle(k, carry):
        vt_ref[k] = N_EXP_TILES - 1
        ve_ref[k] = last_e
        vlo_ref[k] = 0
        vhi_ref[k] = 0
        return carry

    lax.fori_loop(n_vis, N_VISITS, add_idle, 0)
    pos_ref[...] = jnp.concatenate([pos1, pos2, jnp.zeros((6, M_TOK), jnp.int32)], axis=0)


def _plan(cnt, idx):
    smem = pl.BlockSpec(memory_space=pltpu.SMEM)
    vmem = pl.BlockSpec(memory_space=pltpu.VMEM)
    table = jax.ShapeDtypeStruct((N_VISITS,), jnp.int32)
    return pl.pallas_call(
        _plan_body,
        in_specs=[smem, vmem],
        out_specs=[vmem, smem, smem, smem, smem, smem],
        out_shape=[jax.ShapeDtypeStruct((8, M_TOK), jnp.int32), table, table, table, table,
                   jax.ShapeDtypeStruct((1,), jnp.int32)],
        name="moe_plan",
    )(cnt, idx)


def _row_copy(src_ref, src_row, dst_ref, dst_row, sem):
    return pltpu.make_async_copy(src_ref.at[pl.ds(src_row * CHUNKS, CHUNKS)],
                                 dst_ref.at[pl.ds(dst_row * CHUNKS, CHUNKS)], sem)


def _scatter_body(pos1_ref, pos2_ref, x_ref, xs_hbm, stage, sem):
    i = pl.program_id(0)
    last = pl.num_programs(0) - 1
    slot = i % 2

    def wait_slot(s):
        for _ in range(TOP_K):
            pltpu.make_async_copy(stage.at[s], xs_hbm.at[pl.ds(0, MOVE_TM * CHUNKS)], sem.at[s]).wait()

    @pl.when(i >= 2)
    def _():
        wait_slot(slot)

    _to_chunk_major(stage.at[slot], x_ref[...], MOVE_TM)

    def issue(r, carry):
        t = i * MOVE_TM + r
        _row_copy(stage.at[slot], r, xs_hbm, pos1_ref[t], sem.at[slot]).start()
        _row_copy(stage.at[slot], r, xs_hbm, pos2_ref[t], sem.at[slot]).start()
        return carry

    lax.fori_loop(0, MOVE_TM, issue, 0, unroll=ISSUE_UNROLL)

    @pl.when(i == last)
    def _():
        wait_slot(1 - slot)
        wait_slot(slot)


def _scatter(pos1, pos2, x):
    m = x.shape[0]
    assert (m // MOVE_TM) >= 2
    grid_spec = pltpu.PrefetchScalarGridSpec(
        num_scalar_prefetch=2,
        grid=(m // MOVE_TM,),
        in_specs=[pl.BlockSpec((MOVE_TM, D), lambda i, p1, p2: (i, 0))],
        out_specs=pl.BlockSpec(memory_space=pl.ANY),
        scratch_shapes=[pltpu.VMEM((2, MOVE_TM * CHUNKS, LANE), F32),
                        pltpu.SemaphoreType.DMA((2,))])
    return pl.pallas_call(
        _scatter_body,
        grid_spec=grid_spec,
        out_shape=jax.ShapeDtypeStruct((N_ASSIGN * CHUNKS, LANE), F32),
        compiler_params=_cparams(1),
        name="moe_scatter",
    )(pos1, pos2, x)


def _experts_body(vt_ref, ve_ref, vlo_ref, vhi_ref, nvis_ref, x_ref, wg_ref, wu_ref, wd_ref,
                  o_ref, wg_b, wu_b, wd_b):
    v = pl.program_id(0)
    prev = jnp.maximum(v - 1, 0)

    @pl.when(v < nvis_ref[0])
    def _():
        @pl.when(jnp.logical_or(v == 0, ve_ref[v] != ve_ref[prev]))
        def _():
            wg_b[...] = wg_ref[...].astype(BF16)
            wu_b[...] = wu_ref[...].astype(BF16)
            wd_b[...] = wd_ref[...].astype(BF16)

        xb = _from_chunk_major(x_ref, EXP_TM).astype(BF16)
        gate = jnp.dot(xb, wg_b[...], preferred_element_type=F32)
        up = jnp.dot(xb, wu_b[...], preferred_element_type=F32)
        h = (gate * jax.nn.sigmoid(gate) * up).astype(BF16)
        y = jnp.dot(h, wd_b[...], preferred_element_type=F32)
        row = lax.broadcasted_iota(jnp.int32, (EXP_TM, 1), 0)
        mine = jnp.logical_and(row >= vlo_ref[v], row < vhi_ref[v])
        first_visit = jnp.logical_or(v == 0, vt_ref[v] != vt_ref[prev])

        @pl.when(first_visit)
        def _():
            _to_chunk_major(o_ref, jnp.where(mine, y, 0.0), EXP_TM)

        @pl.when(jnp.logical_not(first_visit))
        def _():
            _to_chunk_major(o_ref, jnp.where(mine, y, _from_chunk_major(o_ref, EXP_TM)), EXP_TM)


def _experts(vt, ve, vlo, vhi, nvis, xs, w_gate, w_up, w_down, layer):
    def tile_map(v, vt, ve, vlo, vhi, nvis):
        return (vt[v], 0)

    def weight_map(v, vt, ve, vlo, vhi, nvis):
        return (layer, ve[v], 0, 0)

    grid_spec = pltpu.PrefetchScalarGridSpec(
        num_scalar_prefetch=5,
        grid=(N_VISITS,),
        in_specs=[pl.BlockSpec((EXP_TM * CHUNKS, LANE), tile_map),
                  pl.BlockSpec((None, None, D, D_EXPERT), weight_map),
                  pl.BlockSpec((None, None, D, D_EXPERT), weight_map),
                  pl.BlockSpec((None, None, D_EXPERT, D), weight_map)],
        out_specs=pl.BlockSpec((EXP_TM * CHUNKS, LANE), tile_map),
        scratch_shapes=[pltpu.VMEM((D, D_EXPERT), BF16),
                        pltpu.VMEM((D, D_EXPERT), BF16),
                        pltpu.VMEM((D_EXPERT, D), BF16)])
    return pl.pallas_call(
        _experts_body,
        grid_spec=grid_spec,
        out_shape=jax.ShapeDtypeStruct((N_ASSIGN * CHUNKS, LANE), F32),
        compiler_params=_cparams(1),
        name="moe_experts",
    )(vt, ve, vlo, vhi, nvis, xs, w_gate, w_up, w_down)


def _combine_body(pos1_ref, pos2_ref, x_ref, wts_ref, g_ref, b_ref, ys_hbm, *rest, split):
    out_refs, (ybuf, sem) = rest[:-2], rest[-2:]
    i = pl.program_id(0)
    slot = i % 2

    def gather_rows(step, s):
        def issue(r, carry):
            t = step * MOVE_TM + r
            _row_copy(ys_hbm, pos1_ref[t], ybuf.at[s, 0], r, sem.at[s]).start()
            _row_copy(ys_hbm, pos2_ref[t], ybuf.at[s, 1], r, sem.at[s]).start()
            return carry
        lax.fori_loop(0, MOVE_TM, issue, 0, unroll=ISSUE_UNROLL)

    @pl.when(i == 0)
    def _():
        gather_rows(0, 0)

    @pl.when(i + 1 < pl.num_programs(0))
    def _():
        gather_rows(i + 1, 1 - slot)

    for k in range(TOP_K):
        pltpu.make_async_copy(ys_hbm.at[pl.ds(0, MOVE_TM * CHUNKS)], ybuf.at[slot, k], sem.at[slot]).wait()
    y = (wts_ref[:, 0:1] * _from_chunk_major(ybuf.at[slot, 0], MOVE_TM)
         + wts_ref[:, 1:2] * _from_chunk_major(ybuf.at[slot, 1], MOVE_TM))
    out = _ln(ALPHA * x_ref[...] + y, g_ref[...], b_ref[...])
    if split is None:
        out_refs[0][...] = out
    else:
        @pl.when(i < split)
        def _():
            out_refs[0][...] = out

        @pl.when(i >= split)
        def _():
            out_refs[1][...] = out


def _combine(pos1, pos2, x, wts, g, b, ys, split_rows=None):
    m = x.shape[0]
    if split_rows is None:
        split = None
        out_specs = pl.BlockSpec((MOVE_TM, D), lambda i, p1, p2: (i, 0))
        out_shape = jax.ShapeDtypeStruct((m, D), F32)
    else:
        split = split_rows // MOVE_TM
        out_specs = [pl.BlockSpec((MOVE_TM, D), lambda i, p1, p2: (jnp.minimum(i, split - 1), 0)),
                     pl.BlockSpec((MOVE_TM, D), lambda i, p1, p2: (jnp.maximum(i - split, 0), 0))]
        out_shape = [jax.ShapeDtypeStruct((split_rows, D), F32),
                     jax.ShapeDtypeStruct((m - split_rows, D), F32)]
    grid_spec = pltpu.PrefetchScalarGridSpec(
        num_scalar_prefetch=2,
        grid=(m // MOVE_TM,),
        in_specs=[pl.BlockSpec((MOVE_TM, D), lambda i, p1, p2: (i, 0)),
                  pl.BlockSpec((MOVE_TM, ROUTE_LANES), lambda i, p1, p2: (i, 0)),
                  pl.BlockSpec((1, D), lambda i, p1, p2: (0, 0)),
                  pl.BlockSpec((1, D), lambda i, p1, p2: (0, 0)),
                  pl.BlockSpec(memory_space=pl.ANY)],
        out_specs=out_specs,
        scratch_shapes=[pltpu.VMEM((2, TOP_K, MOVE_TM * CHUNKS, LANE), F32),
                        pltpu.SemaphoreType.DMA((2,))])
    return pl.pallas_call(
        functools.partial(_combine_body, split=split),
        grid_spec=grid_spec,
        out_shape=out_shape,
        compiler_params=_cparams(1),
        name="moe_combine",
    )(pos1, pos2, x, wts, g, b, ys)


def _moe(x, wr, br, w_gate, w_up, w_down, layer, g, b, split_rows=None):
    idx, wts, cnt = _route(x, wr, br)
    pos, vt, ve, vlo, vhi, nvis = _plan(cnt.reshape(ROUTE_LANES), idx)
    xs = _scatter(pos[0], pos[1], x)
    ys = _experts(vt, ve, vlo, vhi, nvis, xs, w_gate, w_up, w_down, layer)
    return _combine(pos[0], pos[1], x, wts, g, b, ys, split_rows)


def _router_params(w_rg, b_rg, w_re, b_re):
    w = jnp.concatenate([w_rg, jnp.transpose(w_re, (1, 0, 2)).reshape(D, N_EXPERTS)], axis=1)
    bias = jnp.concatenate([b_rg, b_re.reshape(N_EXPERTS)])
    pad = ROUTE_LANES - w.shape[1]
    return jnp.pad(w, ((0, 0), (0, pad))), jnp.pad(bias, (0, pad))[None, :]


def kernel(x_prompt, x_sample, mem_prompt, cache_mem_k, cache_mem_v, state_conv3, state_conv31,
           ln_gain, ln_bias, w_in_ab, w_conv3, ln_sgu_gain, ln_sgu_bias, w_spatial, b_spatial, w_out_ab,
           w_pw1, w_dw31, b_dw31, ln_conf_gain, ln_conf_bias, w_pw2,
           w_mem_q, w_mem_k, w_mem_v, w_mem_o,
           w_route_group, b_route_group, w_route_expert, b_route_expert,
           w_exp_gate, w_exp_up, w_exp_down):
    depth = ln_gain.shape[0]
    x = (x_prompt.reshape(N_PROMPT, D), x_sample.reshape(N_STREAMS * STEP, D))

    mem = mem_prompt.reshape(N_MEM, D)
    pk, pk_b = _mem_kv(mem, w_mem_k, "mem_k")
    pv, pv_b = _mem_kv(mem, w_mem_v, "mem_v")
    pk_b = pk_b.reshape(depth, 1, N_MEM, D)
    pv_b = pv_b.reshape(depth, 1, N_MEM, D)
    ck = cache_mem_k.reshape(depth, N_STREAMS, N_MEM, D)
    cv = cache_mem_v.reshape(depth, N_STREAMS, N_MEM, D)

    def ln_params(l, k):
        return ln_gain[l, k][None, :], ln_bias[l, k][None, :]

    c3_prompt, c3_sample, c31_prompt, c31_sample, v_rows = [], [], [], [], []
    for l in range(depth):
        i = l // 2
        if l % 2 == 0:
            y_mix, c3p, c3s, v = _mix0(
                x, w_in_ab, i, state_conv3[i], w_conv3[i], ln_sgu_gain[i][None, :], ln_sgu_bias[i][None, :],
                w_spatial[i], b_spatial[i][:, :, None])
            c3_prompt.append(c3p)
            c3_sample.append(c3s)
            v_rows.append(v)
            w_out = w_out_ab
        else:
            y_mix, c31p, c31s = _mix1(
                x, w_pw1, i, state_conv31[i], w_dw31[i], b_dw31[i][None, :],
                ln_conf_gain[i][None, :], ln_conf_bias[i][None, :])
            c31_prompt.append(c31p)
            c31_sample.append(c31s)
            w_out = w_pw2
        x = _mm_ln(y_mix, w_out, i, x, *ln_params(l, 0), name="mix_out_ln")
        x = _attn_block(x, pk_b, pv_b, ck, cv, *ln_params(l, 1), w_mem_q, w_mem_o, l)
        wr, br = _router_params(w_route_group[l], b_route_group[l], w_route_expert[l], b_route_expert[l])
        x = _moe(x, wr, br, w_exp_gate, w_exp_up, w_exp_down, l, *ln_params(l, 2),
                 split_rows=N_PROMPT if l == depth - 1 else None)

    y_prompt = x[0].reshape(1, N_PROMPT, D)
    y_sample = x[1].reshape(N_STREAMS, STEP, D)
    prompt_mem_k = pk.reshape(depth, 1, N_MEM, HEADS, HEAD_DIM)
    prompt_mem_v = pv.reshape(depth, 1, N_MEM, HEADS, HEAD_DIM)
    prompt_conv3 = jnp.stack(c3_prompt)
    sample_conv3 = jnp.stack(c3_sample)
    prompt_conv31 = jnp.stack(c31_prompt)
    sample_conv31 = jnp.stack(c31_sample)
    sample_sgu_v = jnp.stack(v_rows)
    return (y_prompt, y_sample, prompt_mem_k, prompt_mem_v, prompt_conv3, prompt_conv31,
            sample_conv3, sample_conv31, sample_sgu_v)
```
